```python
import jax
import jax.numpy as jnp
from jax import lax
import numpy as np

D_MODEL = 2048
BATCH = 4
SEQ = 4096
DEPTH = 2

PLE_DIM = 256
HEAD_DIM = 64
BLOCK = 128
LN_EPS = 1e-5
D_FF = 5632
NEG_BIG = -1e30
F_MIN = 1e-6

A_Q_HEADS = 8
A_KV_HEADS = 2
A_GROUP = A_Q_HEADS // A_KV_HEADS
A_WINDOW = 128
A_WIDTH = A_Q_HEADS * HEAD_DIM
A_KV_WIDTH = A_KV_HEADS * HEAD_DIM

B_GROUPS = 4
B_GROUP_DIM = 128
B_CHUNK = 128
B_WIDTH = B_GROUPS * B_GROUP_DIM

C_HEADS = 4
C_KEY_DIM = 128
C_VAL_DIM = 128
C_WIDTH = C_HEADS * C_KEY_DIM
C_CHUNK = 64

D_PATTERNS = ((128, 1), (512, 4), (2048, 16))
D_N_GROUPS = 3
D_HEADS_PER_GROUP = 4
D_GROUP_WIDTH = D_HEADS_PER_GROUP * HEAD_DIM
D_WIDTH = D_N_GROUPS * D_GROUP_WIDTH

N_SOFTMAX_HEADS = A_Q_HEADS + D_N_GROUPS * D_HEADS_PER_GROUP
N_BRANCHES = 4
IN_SIZES = (A_WIDTH, A_KV_WIDTH, A_KV_WIDTH, B_WIDTH, B_WIDTH, C_WIDTH, C_WIDTH, C_WIDTH, C_WIDTH, D_WIDTH, D_WIDTH, D_WIDTH, N_BRANCHES * D_MODEL)
IN_WIDTH = A_WIDTH + 2 * A_KV_WIDTH + 2 * B_WIDTH + 4 * C_WIDTH + 3 * D_WIDTH + N_BRANCHES * D_MODEL

F32 = jnp.float32

kernel_name = 'hybrid_gated_parallel_mixer_deepnorm'


def split_cols(z, sizes):
    out = []
    off = 0
    for sz in sizes:
        out.append(z[..., off:off + sz])
        off += sz
    return out


def layer_norm(x, g, b, eps=LN_EPS):
    xf = x.astype(F32)
    mu = jnp.mean(xf, axis=-1, keepdims=True)
    var = jnp.mean(jnp.square(xf - mu), axis=-1, keepdims=True)
    y = (xf - mu) * lax.rsqrt(var + eps) * g.astype(F32) + b.astype(F32)
    return y.astype(x.dtype)


def rms_norm(x, g, eps=LN_EPS):
    xf = x.astype(F32)
    y = xf * lax.rsqrt(jnp.mean(jnp.square(xf), axis=-1, keepdims=True) + eps) * g.astype(F32)
    return y.astype(x.dtype)


def swiglu(x, w_gate, w_up, w_down):
    return (jax.nn.silu(x @ w_gate) * (x @ w_up)) @ w_down


def alibi_slopes(n):
    return jnp.exp2(-8.0 * jnp.arange(1, n + 1, dtype=F32) / n)


def banded_attention(q, k, v, slopes, max_dist, dist_unit, sinks=None):
    n, length, hkv, grp, dh = q.shape
    nb = length // BLOCK
    qb = q.reshape(n, nb, BLOCK, hkv, grp, dh)

    def with_prev(t):
        tb = t.reshape(n, nb, BLOCK, hkv, dh)
        prev = jnp.pad(tb[:, :-1], ((0, 0), (1, 0), (0, 0), (0, 0), (0, 0)))
        return jnp.concatenate([prev, tb], axis=2)

    kk = with_prev(k)
    vv = with_prev(v)
    s = jnp.einsum('nbqhgd,nbkhd->nbhgqk', qb, kk).astype(F32) * (dh ** -0.5)
    q_off = jnp.arange(BLOCK)[:, None]
    k_off = jnp.arange(2 * BLOCK)[None, :]
    dist = q_off + BLOCK - k_off
    blk = jnp.arange(nb)[:, None, None]
    mask = (dist >= 0) & (dist <= max_dist) & (blk * BLOCK + k_off - BLOCK >= 0)
    dist_pos = jnp.maximum(dist, 0) * dist_unit
    bias = -slopes.astype(F32)[:, :, None, None] * dist_pos.astype(F32)
    s = jnp.where(mask[None, :, None, None], s + bias[None, None], NEG_BIG)
    m = jnp.max(s, axis=-1)
    if sinks is not None:
        sk = sinks.astype(F32)[None, None, :, :, None]
        m = jnp.maximum(m, sk)
    pr = jnp.exp(s - m[..., None])
    denom = jnp.sum(pr, axis=-1)
    if sinks is not None:
        denom = denom + jnp.exp(sk - m)
    o = jnp.einsum('nbhgqk,nbkhd->nbqhgd', pr, vv.astype(F32))
    o = o * jnp.transpose(1.0 / denom, (0, 1, 4, 2, 3))[..., None]
    lse = jnp.transpose(m + jnp.log(denom), (0, 1, 4, 2, 3))
    return o.reshape(n, length, hkv, grp, dh).astype(q.dtype), lse.reshape(n, length, hkv, grp)


def dilated_mixture(q, k, v, slopes):
    b, s = q.shape[:2]
    outs = []
    lses = []
    for g, (window, dil) in enumerate(D_PATTERNS):
        sub = s // dil
        sub_pad = -(-sub // BLOCK) * BLOCK

        def to_res(t):
            t = t.reshape(b, sub, dil, D_HEADS_PER_GROUP, HEAD_DIM).transpose(0, 2, 1, 3, 4)
            t = t.reshape(b * dil, sub, D_HEADS_PER_GROUP, HEAD_DIM)
            return jnp.pad(t, ((0, 0), (0, sub_pad - sub), (0, 0), (0, 0)))

        qr = to_res(q[:, :, g])
        kr = to_res(k[:, :, g])
        vr = to_res(v[:, :, g])
        o, lse = banded_attention(qr[:, :, :, None], kr, vr, slopes[g][:, None], window // dil, dil)
        o = o[:, :sub, :, 0].reshape(b, dil, sub, D_HEADS_PER_GROUP, HEAD_DIM)
        o = o.transpose(0, 2, 1, 3, 4).reshape(b, s, D_HEADS_PER_GROUP, HEAD_DIM)
        lse = lse[:, :sub, :, 0].reshape(b, dil, sub, D_HEADS_PER_GROUP)
        lse = lse.transpose(0, 2, 1, 3).reshape(b, s, D_HEADS_PER_GROUP)
        outs.append(o)
        lses.append(lse)
    w = jax.nn.softmax(jnp.stack(lses, axis=0), axis=0)
    out = jnp.sum(w[..., None] * jnp.stack(outs, axis=0).astype(F32), axis=0)
    return out.reshape(b, s, D_GROUP_WIDTH).astype(q.dtype)


def chunked_spatial_gating(u_pre, v_pre, ln_g, ln_b, w_s, b_s):
    b, s, _ = u_pre.shape
    u = jax.nn.gelu(u_pre, approximate=False)
    v = layer_norm(jax.nn.gelu(v_pre, approximate=False), ln_g, ln_b)
    v = v.reshape(b, s // B_CHUNK, B_CHUNK, B_GROUPS, B_GROUP_DIM)
    w_causal = jnp.tril(w_s)
    mixed = jnp.einsum('gts,bnsgc->bntgc', w_causal, v) + jnp.transpose(b_s)[None, None, :, :, None]
    return u * mixed.reshape(b, s, B_WIDTH)


def hgrn2_chunk_scan(q, k, v, log_f):
    b, s, h, dk = q.shape
    dv = v.shape[-1]
    n = s // C_CHUNK

    def chunks(t):
        return t.astype(F32).reshape(b, n, C_CHUNK, h, t.shape[-1]).transpose(1, 0, 3, 2, 4)

    qc, kc, vc, gc = chunks(q), chunks(k), chunks(v), chunks(log_f)
    causal = jnp.tril(jnp.ones((C_CHUNK, C_CHUNK), dtype=bool))[:, :, None]

    def step(state, inp):
        qi, ki, vi, gi = inp
        G = jnp.cumsum(gi, axis=2)
        rel = G[:, :, :, None, :] - G[:, :, None, :, :]
        decay = jnp.where(causal, jnp.exp(jnp.minimum(rel, 0.0)), 0.0)
        scores = jnp.einsum('bhtk,bhtsk,bhsk->bhts', qi, decay, ki)
        o = jnp.einsum('bhts,bhsv->bhtv', scores, vi) + jnp.einsum('bhtk,bhkv->bhtv', qi * jnp.exp(G), state)
        g_last = G[:, :, -1:, :]
        state = state * jnp.exp(G[:, :, -1])[..., None] + jnp.einsum('bhsk,bhsv->bhkv', ki * jnp.exp(g_last - G), vi)
        return state, o

    state0 = jnp.zeros((b, h, dk, dv), F32)
    _, o = lax.scan(step, state0, (qc, kc, vc, gc))
    return o.transpose(1, 0, 3, 2, 4).reshape(b, s, h, dv)


def hgrn2(q, f_pre, i_in, og_pre, lower_bound, norm_g):
    b, s, _ = q.shape
    lb = lower_bound.astype(F32)
    z = f_pre.astype(F32)
    f = lb + (1.0 - lb) * jax.nn.sigmoid(z)
    log_f = jnp.log(jnp.maximum(f, F_MIN))
    k = (1.0 - lb) * jax.nn.sigmoid(-z)
    o = hgrn2_chunk_scan(q.reshape(b, s, C_HEADS, C_KEY_DIM), k.reshape(b, s, C_HEADS, C_KEY_DIM),
                         i_in.reshape(b, s, C_HEADS, C_VAL_DIM), log_f.reshape(b, s, C_HEADS, C_KEY_DIM))
    o = rms_norm(o, norm_g.reshape(C_HEADS, C_VAL_DIM)).reshape(b, s, C_WIDTH)
    return (o * jax.nn.sigmoid(og_pre.astype(F32))).astype(q.dtype)


def hybrid_mixer(h, w_in, sinks, gmlp_ln_g, gmlp_ln_b, gmlp_w_s, gmlp_b_s, lower_bound, hgrn_norm_g,
                 w_br_a, w_br_b, w_br_c, w_br_d, w_out):
    b, s, _ = h.shape
    z = h @ w_in
    qa, ka, va, ub, vb, qc, fc, ic, gc, qd, kd, vd, gate_logits = split_cols(z, IN_SIZES)
    slopes = alibi_slopes(N_SOFTMAX_HEADS)
    o_a, _ = banded_attention(qa.reshape(b, s, A_KV_HEADS, A_GROUP, HEAD_DIM),
                              ka.reshape(b, s, A_KV_HEADS, HEAD_DIM),
                              va.reshape(b, s, A_KV_HEADS, HEAD_DIM),
                              slopes[:A_Q_HEADS].reshape(A_KV_HEADS, A_GROUP),
                              A_WINDOW - 1, 1, sinks.reshape(A_KV_HEADS, A_GROUP))
    o_a = o_a.reshape(b, s, A_WIDTH)
    o_b = chunked_spatial_gating(ub, vb, gmlp_ln_g, gmlp_ln_b, gmlp_w_s, gmlp_b_s)
    o_c = hgrn2(qc, fc, ic, gc, lower_bound, hgrn_norm_g)
    dshape = (b, s, D_N_GROUPS, D_HEADS_PER_GROUP, HEAD_DIM)
    o_d = dilated_mixture(qd.reshape(dshape), kd.reshape(dshape), vd.reshape(dshape),
                          slopes[A_Q_HEADS:].reshape(D_N_GROUPS, D_HEADS_PER_GROUP))
    gates = jax.nn.sigmoid(gate_logits.astype(F32)).reshape(b, s, N_BRANCHES, D_MODEL)
    merged = (gates[:, :, 0] * (o_a @ w_br_a) + gates[:, :, 1] * (o_b @ w_br_b)
              + gates[:, :, 2] * (o_c @ w_br_c) + gates[:, :, 3] * (o_d @ w_br_d))
    return merged.astype(h.dtype) @ w_out


def setup_inputs(seed: int = 0) -> dict:
    key = jax.random.key(seed)
    ks = jax.random.split(key, 24)
    beta = (8.0 * DEPTH) ** -0.25

    def nrm(k, shape, scale):
        return jax.random.normal(k, shape, F32) * scale

    return {
        'x': nrm(ks[0], (BATCH, SEQ, D_MODEL), 1.0),
        'p': nrm(ks[1], (DEPTH, BATCH, SEQ, PLE_DIM), 1.0),
        'ln_g': 1.0 + nrm(ks[2], (DEPTH, 3, D_MODEL), 0.02),
        'ln_b': nrm(ks[3], (DEPTH, 3, D_MODEL), 0.02),
        'ffn_w_gate': nrm(ks[4], (DEPTH, 2, D_MODEL, D_FF), D_MODEL ** -0.5),
        'ffn_w_up': nrm(ks[5], (DEPTH, 2, D_MODEL, D_FF), D_MODEL ** -0.5),
        'ffn_w_down': nrm(ks[6], (DEPTH, 2, D_FF, D_MODEL), beta * D_FF ** -0.5),
        'w_in': nrm(ks[7], (DEPTH, D_MODEL, IN_WIDTH), D_MODEL ** -0.5),
        'attn_sinks': nrm(ks[8], (DEPTH, A_Q_HEADS), 0.5),
        'gmlp_ln_g': 1.0 + nrm(ks[9], (DEPTH, B_WIDTH), 0.02),
        'gmlp_ln_b': nrm(ks[10], (DEPTH, B_WIDTH), 0.02),
        'gmlp_w_s': nrm(ks[11], (DEPTH, B_GROUPS, B_CHUNK, B_CHUNK), B_CHUNK ** -0.5),
        'gmlp_b_s': 1.0 + nrm(ks[12], (DEPTH, B_GROUPS, B_CHUNK), 0.02),
        'hgrn_lb_logits': nrm(ks[13], (DEPTH, C_WIDTH), 0.5),
        'hgrn_norm_g': 1.0 + nrm(ks[14], (DEPTH, C_WIDTH), 0.02),
        'w_br_a': nrm(ks[15], (DEPTH, A_WIDTH, D_MODEL), A_WIDTH ** -0.5),
        'w_br_b': nrm(ks[16], (DEPTH, B_WIDTH, D_MODEL), B_WIDTH ** -0.5),
        'w_br_c': nrm(ks[17], (DEPTH, C_WIDTH, D_MODEL), C_WIDTH ** -0.5),
        'w_br_d': nrm(ks[18], (DEPTH, D_GROUP_WIDTH, D_MODEL), D_GROUP_WIDTH ** -0.5),
        'w_out': nrm(ks[19], (DEPTH, D_MODEL, D_MODEL), beta * D_MODEL ** -0.5),
        'ple_w_proj': nrm(ks[20], (DEPTH, PLE_DIM, D_MODEL), beta * PLE_DIM ** -0.5),
        'ple_w_gate': nrm(ks[21], (DEPTH, D_MODEL, D_MODEL), D_MODEL ** -0.5),
    }


def reference(x, p, ln_g, ln_b, ffn_w_gate, ffn_w_up, ffn_w_down, w_in, attn_sinks, gmlp_ln_g, gmlp_ln_b,
              gmlp_w_s, gmlp_b_s, hgrn_lb_logits, hgrn_norm_g, w_br_a, w_br_b, w_br_c, w_br_d, w_out,
              ple_w_proj, ple_w_gate):
    alpha = (2.0 * DEPTH) ** 0.25
    probs = jax.nn.softmax(hgrn_lb_logits.astype(F32), axis=0)
    lower_bounds = jnp.cumsum(probs.at[0].set(0.0), axis=0)
    for i in range(DEPTH):
        ffn1 = swiglu(x, ffn_w_gate[i, 0], ffn_w_up[i, 0], ffn_w_down[i, 0])
        x = layer_norm(alpha * x + 0.5 * ffn1, ln_g[i, 0], ln_b[i, 0])
        mix = hybrid_mixer(x, w_in[i], attn_sinks[i], gmlp_ln_g[i], gmlp_ln_b[i], gmlp_w_s[i], gmlp_b_s[i],
                           lower_bounds[i], hgrn_norm_g[i], w_br_a[i], w_br_b[i], w_br_c[i], w_br_d[i], w_out[i])
        x = layer_norm(alpha * x + mix, ln_g[i, 1], ln_b[i, 1])
        ffn2 = swiglu(x, ffn_w_gate[i, 1], ffn_w_up[i, 1], ffn_w_down[i, 1])
        ple = jax.nn.sigmoid(x @ ple_w_gate[i]) * (p[i] @ ple_w_proj[i])
        x = layer_norm(alpha * x + 0.5 * ffn2 + ple, ln_g[i, 2], ln_b[i, 2])
    return x
```

```python
import functools

import numpy as np

import jax
import jax.numpy as jnp
from jax import lax
from jax.experimental import pallas as pl
from jax.experimental.pallas import tpu as pltpu

F32 = jnp.float32
BF16 = jnp.bfloat16

LANES = 128
VMEM_LIMIT_BYTES = 56 * 1024 * 1024

HEAD_DIM = 64
BLOCK = 128
LN_EPS = 1e-5
NEG_BIG = -1e30
F_MIN = 1e-6
A_Q_HEADS = 8
A_KV_HEADS = 2
A_WINDOW = 128
B_GROUPS = 4
B_CHUNK = 128
C_HEADS = 4
C_DIM = 128
D_PATTERNS = ((128, 1), (512, 4), (2048, 16))
D_HEADS = 4
N_SOFTMAX_HEADS = A_Q_HEADS + len(D_PATTERNS) * D_HEADS
N_BRANCHES = 4
A_WIDTH = A_Q_HEADS * HEAD_DIM
A_KV_WIDTH = A_KV_HEADS * HEAD_DIM
B_WIDTH = B_GROUPS * B_CHUNK
C_WIDTH = C_HEADS * C_DIM
D_GROUP_WIDTH = D_HEADS * HEAD_DIM
D_WIDTH = len(D_PATTERNS) * D_GROUP_WIDTH
COL_QA = 0
COL_KA = COL_QA + A_WIDTH
COL_VA = COL_KA + A_KV_WIDTH
COL_UB = COL_VA + A_KV_WIDTH
COL_VB = COL_UB + B_WIDTH
COL_QC = COL_VB + B_WIDTH
COL_FC = COL_QC + C_WIDTH
COL_IC = COL_FC + C_WIDTH
COL_GC = COL_IC + C_WIDTH
COL_QD = COL_GC + C_WIDTH
COL_KD = COL_QD + D_WIDTH
COL_VD = COL_KD + D_WIDTH
MIX_WIDTH = COL_VD + D_WIDTH

HGRN_CHUNK = 128
HGRN_LEVELS = (64, 32, 16, 8, 4, 2, 1)


def _cparams(*sem):
    return pltpu.CompilerParams(dimension_semantics=sem, vmem_limit_bytes=VMEM_LIMIT_BYTES)


def _layer_norm(v, g, b):
    mu = jnp.mean(v, axis=-1, keepdims=True)
    d = v - mu
    var = jnp.mean(d * d, axis=-1, keepdims=True)
    return d * lax.rsqrt(var + LN_EPS) * g + b


def _dot(a, b):
    return jnp.dot(a, b, preferred_element_type=F32)


def _dot_nt(a, b):
    return lax.dot_general(a, b, (((1,), (1,)), ((), ())), preferred_element_type=F32)


def _dot_tn(a, b):
    return lax.dot_general(a, b, (((0,), (0,)), ((), ())), preferred_element_type=F32)


def _ffn_kernel(x_ref, wg_ref, wu_ref, wd_ref, g_ref, b_ref, *rest, alpha, has_extra):
    if has_extra:
        e_ref, o_ref, ob_ref, xb_ref, acc_ref = rest
    else:
        o_ref, ob_ref, xb_ref, acc_ref = rest
    f = pl.program_id(1)

    @pl.when(f == 0)
    def _():
        xb_ref[...] = x_ref[...].astype(BF16)
        acc_ref[...] = jnp.zeros_like(acc_ref)

    xb = xb_ref[...]
    g = _dot(xb, wg_ref[...])
    u = _dot(xb, wu_ref[...])
    h = (g * jax.nn.sigmoid(g) * u).astype(BF16)
    acc_ref[...] += _dot(h, wd_ref[...])

    @pl.when(f == pl.num_programs(1) - 1)
    def _():
        y = alpha * x_ref[...] + 0.5 * acc_ref[...]
        if has_extra:
            y = y + e_ref[...]
        o = _layer_norm(y, g_ref[...], b_ref[...])
        o_ref[...] = o
        ob_ref[...] = o.astype(BF16)


def _ffn(x, wg, wu, wd, layer, which, ln_g, ln_b, extra, alpha):
    t, d = x.shape
    ff = wg.shape[-1]
    tm = min(512, t)
    tf = min(512, ff)
    row = pl.BlockSpec((tm, d), lambda m, f: (m, 0))
    vec = pl.BlockSpec((1, d), lambda m, f: (0, 0))
    in_specs = [
        row,
        pl.BlockSpec((None, None, d, tf), lambda m, f: (layer, which, 0, f)),
        pl.BlockSpec((None, None, d, tf), lambda m, f: (layer, which, 0, f)),
        pl.BlockSpec((None, None, tf, d), lambda m, f: (layer, which, f, 0)),
        vec,
        vec,
    ]
    args = [x, wg, wu, wd, ln_g, ln_b]
    if extra is not None:
        in_specs.append(row)
        args.append(extra)
    return pl.pallas_call(
        functools.partial(_ffn_kernel, alpha=alpha, has_extra=extra is not None),
        grid=(t // tm, ff // tf),
        in_specs=in_specs,
        out_specs=[row, row],
        out_shape=[jax.ShapeDtypeStruct((t, d), F32), jax.ShapeDtypeStruct((t, d), BF16)],
        scratch_shapes=[pltpu.VMEM((tm, d), BF16), pltpu.VMEM((tm, d), F32)],
        compiler_params=_cparams("parallel", "arbitrary"),
        name="ffn_ln",
    )(*args)


def _proj_kernel(x_ref, w_ref, o_ref, *, act):
    z = _dot(x_ref[...], w_ref[...])
    if act:
        z = jax.nn.sigmoid(z)
    o_ref[...] = z.astype(o_ref.dtype)


def _proj(xb, w, layer, col0, width, out_dtype, act):
    t, d = xb.shape
    tm = min(2048, t)
    tn = min(1024, width)
    off = col0 // tn
    return pl.pallas_call(
        functools.partial(_proj_kernel, act=act),
        grid=(width // tn, t // tm),
        in_specs=[
            pl.BlockSpec((tm, d), lambda n, m: (m, 0)),
            pl.BlockSpec((None, d, tn), lambda n, m: (layer, 0, n + off)),
        ],
        out_specs=pl.BlockSpec((tm, tn), lambda n, m: (m, n)),
        out_shape=jax.ShapeDtypeStruct((t, width), out_dtype),
        compiler_params=_cparams("parallel", "arbitrary"),
        name="in_proj_gates" if act else "in_proj_mix",
    )(xb, w)


def _band_attn_kernel(slope_ref, sink_ref, q_ref, k_ref, v_ref, kp_ref, vp_ref, *rest,
                      dil, nblk, max_dist, sb_per_seq, gqa, use_sinks, want_lse):
    if want_lse:
        o_ref, lse_ref, kbuf, vbuf = rest
    else:
        o_ref, kbuf, vbuf = rest
        lse_ref = None
    i = pl.program_id(0)
    p = pl.program_id(1)
    bw = BLOCK * dil
    first_in_seq = (i % sb_per_seq) == 0
    lane = lax.broadcasted_iota(jnp.int32, (1, LANES), 1)
    low = lane < HEAD_DIM

    def stage(cur_ref, prev_ref, buf):
        cur = cur_ref[...]
        prev = prev_ref[...]
        if gqa:
            kv_first = (p // 2) == 0

            def dup(a):
                r = pltpu.roll(a, HEAD_DIM, 1)
                return jnp.where(kv_first, jnp.where(low, a, r), jnp.where(low, r, a))

            cur, prev = dup(cur), dup(prev)
        buf[0:bw, :] = prev
        buf[bw:, :] = cur

    stage(k_ref, kp_ref, kbuf)
    stage(v_ref, vp_ref, vbuf)

    q_off = lax.broadcasted_iota(jnp.int32, (2 * BLOCK, 2 * BLOCK), 0) % BLOCK
    k_off = lax.broadcasted_iota(jnp.int32, (2 * BLOCK, 2 * BLOCK), 1)
    dist = q_off + BLOCK - k_off
    in_band = (dist >= 0) & (dist <= max_dist)
    is_cur = k_off >= BLOCK
    dist_f = (jnp.maximum(dist, 0) * dil).astype(F32)
    top = lax.broadcasted_iota(jnp.int32, (2 * BLOCK, 1), 0) < BLOCK
    neg_slope = jnp.where(top, -slope_ref[2 * p], -slope_ref[2 * p + 1])
    bias = neg_slope * dist_f
    if use_sinks:
        sink = jnp.where(top, sink_ref[2 * p], sink_ref[2 * p + 1])

    def body(it, carry):
        j = it // dil
        c = it % dil
        row0 = j * bw + c
        if dil == 1:
            row0 = pl.multiple_of(row0, BLOCK)
            qs = pl.ds(row0, BLOCK)
            ks = pl.ds(row0, 2 * BLOCK)
        else:
            qs = pl.ds(row0, BLOCK, stride=dil)
            ks = pl.ds(row0, 2 * BLOCK, stride=dil)
        qc = q_ref[qs, :] * (HEAD_DIM ** -0.5)
        kk = kbuf[ks, :]
        vv = vbuf[ks, :]
        q2 = jnp.concatenate([jnp.where(low, qc, 0.0), jnp.where(low, 0.0, qc)], axis=0).astype(BF16)
        s = _dot_nt(q2, kk.astype(BF16)) + bias
        has_prev = jnp.logical_or(j > 0, jnp.logical_not(first_in_seq))
        valid = in_band & (is_cur | has_prev)
        s = jnp.where(valid, s, NEG_BIG)
        m = jnp.max(s, axis=-1, keepdims=True)
        if use_sinks:
            m = jnp.maximum(m, sink)
        pr = jnp.exp(s - m)
        den = jnp.sum(pr, axis=-1, keepdims=True)
        if use_sinks:
            den = den + jnp.exp(sink - m)
        prb = pr.astype(BF16)
        o0 = _dot(prb[0:BLOCK], jnp.where(low, vv, 0.0).astype(BF16))
        o1 = _dot(prb[BLOCK:], jnp.where(low, 0.0, vv).astype(BF16))
        inv = 1.0 / den
        o = o0 * inv[0:BLOCK] + o1 * inv[BLOCK:]
        o_ref[qs, :] = o.astype(o_ref.dtype)
        if want_lse:
            lse = m + jnp.log(den)
            lse_ref[qs, :] = jnp.where(low, lse[0:BLOCK], lse[BLOCK:])
        return carry

    lax.fori_loop(0, nblk * dil, body, 0)


def _band_attn(z, slopes, sinks, *, seq, q_col, k_col, v_col, n_pairs, dil, max_dist, gqa, use_sinks,
               want_lse, out_dtype):
    t = z.shape[0]
    bw = BLOCK * dil
    sb = min(max(bw, 1024), seq)
    nblk = sb // bw
    qb, kb, vb = q_col // LANES, k_col // LANES, v_col // LANES
    if gqa:
        kv_map = lambda i, p: (i, kb)
        vv_map = lambda i, p: (i, vb)
        kp_map = lambda i, p: (jnp.maximum(i * nblk - 1, 0), kb)
        vp_map = lambda i, p: (jnp.maximum(i * nblk - 1, 0), vb)
    else:
        kv_map = lambda i, p: (i, kb + p)
        vv_map = lambda i, p: (i, vb + p)
        kp_map = lambda i, p: (jnp.maximum(i * nblk - 1, 0), kb + p)
        vp_map = lambda i, p: (jnp.maximum(i * nblk - 1, 0), vb + p)
    smem = pl.BlockSpec(memory_space=pltpu.SMEM)
    out_spec = pl.BlockSpec((sb, LANES), lambda i, p: (i, p))
    out_specs = [out_spec]
    out_shape = [jax.ShapeDtypeStruct((t, n_pairs * LANES), out_dtype)]
    if want_lse:
        out_specs.append(out_spec)
        out_shape.append(jax.ShapeDtypeStruct((t, n_pairs * LANES), F32))
    return pl.pallas_call(
        functools.partial(_band_attn_kernel, dil=dil, nblk=nblk, max_dist=max_dist, sb_per_seq=seq // sb,
                          gqa=gqa, use_sinks=use_sinks, want_lse=want_lse),
        grid=(t // sb, n_pairs),
        in_specs=[
            smem,
            smem,
            pl.BlockSpec((sb, LANES), lambda i, p: (i, qb + p)),
            pl.BlockSpec((sb, LANES), kv_map),
            pl.BlockSpec((sb, LANES), vv_map),
            pl.BlockSpec((bw, LANES), kp_map),
            pl.BlockSpec((bw, LANES), vp_map),
        ],
        out_specs=out_specs,
        out_shape=out_shape,
        scratch_shapes=[pltpu.VMEM((bw + sb, LANES), F32), pltpu.VMEM((bw + sb, LANES), F32)],
        compiler_params=_cparams("parallel", "arbitrary"),
        name=f"band_attn_d{dil}" + ("_gqa" if gqa else ""),
    )(slopes, sinks, z, z, z, z, z)


def _gelu(x):
    return 0.5 * x * (1.0 + lax.erf(x * np.float32(np.sqrt(0.5))))


def _gmlp_kernel(u0_ref, u1_ref, v0_ref, v1_ref, g_ref, b_ref, ws_ref, bs_ref, o_ref):
    rows = u0_ref.shape[0]
    u = _gelu(jnp.concatenate([u0_ref[...], u1_ref[...]], axis=1))
    v = _gelu(jnp.concatenate([v0_ref[...], v1_ref[...]], axis=1))
    v = _layer_norm(v, g_ref[...], b_ref[...]).astype(BF16)
    t_idx = lax.broadcasted_iota(jnp.int32, (B_CHUNK, B_CHUNK), 0)
    s_idx = lax.broadcasted_iota(jnp.int32, (B_CHUNK, B_CHUNK), 1)
    causal = t_idx >= s_idx
    for grp in range(B_GROUPS):
        w = jnp.where(causal, ws_ref[grp], 0.0).astype(BF16)
        bias = bs_ref[:, grp:grp + 1]
        cols = slice(grp * B_CHUNK, (grp + 1) * B_CHUNK)
        for ch in range(rows // B_CHUNK):
            rs = slice(ch * B_CHUNK, (ch + 1) * B_CHUNK)
            mixed = _dot(w, v[rs, cols]) + bias
            o_ref[rs, cols] = (u[rs, cols] * mixed).astype(o_ref.dtype)


def _gmlp(z, ln_g, ln_b, w_s, b_s_t):
    t = z.shape[0]
    rows = min(512, t)
    half = B_WIDTH // 2
    ub, vb = COL_UB // half, COL_VB // half
    col = lambda j: pl.BlockSpec((rows, half), lambda i: (i, j))
    vec = pl.BlockSpec((1, B_WIDTH), lambda i: (0, 0))
    return pl.pallas_call(
        _gmlp_kernel,
        grid=(t // rows,),
        in_specs=[col(ub), col(ub + 1), col(vb), col(vb + 1), vec, vec,
                  pl.BlockSpec((B_GROUPS, B_CHUNK, B_CHUNK), lambda i: (0, 0, 0)),
                  pl.BlockSpec((B_CHUNK, B_GROUPS), lambda i: (0, 0))],
        out_specs=pl.BlockSpec((rows, B_WIDTH), lambda i: (i, 0)),
        out_shape=jax.ShapeDtypeStruct((t, B_WIDTH), BF16),
        compiler_params=_cparams("parallel"),
        name="spatial_gating",
    )(z, z, z, z, ln_g, ln_b, w_s, b_s_t)


def _hgrn_tables():
    n = HGRN_CHUNK
    t = np.arange(n)[:, None]
    i = np.arange(n)[None, :]
    tabs = []
    for h in HGRN_LEVELS:
        start = (t // h) * h
        upper = ((t // h) % 2) == 1
        q_tab = (i >= start) & (i <= t)
        k_tab = (i > t) & (i < start + h)
        tabs.append(np.where(upper, q_tab, k_tab))
    tabs.append(i <= t)
    tabs.append(i > t)
    w = np.concatenate(tabs, axis=0).astype(np.float32)
    s = np.arange(n)[None, :]
    x = np.bitwise_xor(t, s)
    level = np.where(s < t, np.floor(np.log2(np.maximum(x, 1))).astype(np.int32), np.where(s == t, -1, -2))
    return w, level.astype(np.int32)


def _hgrn_kernel(lbl_ref, ng_ref, w_ref, lvl_ref, q_ref, f_ref, i_ref, g_ref, o_ref, st_ref, *, layer):
    n = HGRN_CHUNK

    @pl.when(pl.program_id(2) == 0)
    def _():
        st_ref[...] = jnp.zeros_like(st_ref)

    logits = lbl_ref[...]
    e = jnp.exp(logits - jnp.max(logits, axis=0, keepdims=True))
    probs = e / jnp.sum(e, axis=0, keepdims=True)
    lb = jnp.zeros((1, C_DIM), F32)
    for j in range(1, layer + 1):
        lb = lb + probs[j:j + 1, :]

    w = w_ref[...]
    lvl = lvl_ref[...]
    row = lax.broadcasted_iota(jnp.int32, (n, C_DIM), 0)
    ng = ng_ref[...]
    for ch in range(q_ref.shape[0] // n):
        rs = slice(ch * n, (ch + 1) * n)
        z = f_ref[rs, :]
        q = q_ref[rs, :]
        f = lb + (1.0 - lb) * jax.nn.sigmoid(z)
        logf = jnp.log(jnp.maximum(f, F_MIN))
        kk = (1.0 - lb) * jax.nn.sigmoid(-z)
        vb = i_ref[rs, :].astype(BF16)
        l1 = logf.astype(BF16)
        r1 = logf - l1.astype(F32)
        l2 = r1.astype(BF16)
        l3 = (r1 - l2.astype(F32)).astype(BF16)
        expo = _dot(w, l1) + _dot(w, l2) + _dot(w, l3)
        scores = jnp.where(lvl == -1, _dot_nt(q.astype(BF16), kk.astype(BF16)), 0.0)
        for li, h in enumerate(HGRN_LEVELS):
            upper = (row & h) != 0
            xh = (jnp.where(upper, q, kk) * jnp.exp(expo[li * n:(li + 1) * n])).astype(BF16)
            scores = scores + jnp.where(lvl == int(np.log2(h)), _dot_nt(xh, xh), 0.0)
        nl = len(HGRN_LEVELS)
        g_incl = expo[nl * n:(nl + 1) * n]
        g_rest = expo[(nl + 1) * n:(nl + 2) * n]
        st = st_ref[...]
        o = _dot(scores.astype(BF16), vb) + _dot_nt((q * jnp.exp(g_incl)).astype(BF16), st.astype(BF16))
        st_ref[...] = st * jnp.exp(g_incl[n - 1:n, :]) + _dot_tn(vb, (kk * jnp.exp(g_rest)).astype(BF16))
        o = o * lax.rsqrt(jnp.mean(o * o, axis=-1, keepdims=True) + LN_EPS) * ng
        o_ref[rs, :] = (o * jax.nn.sigmoid(g_ref[rs, :])).astype(o_ref.dtype)


def _hgrn(z, lb_logits, norm_g, layer, batch, seq):
    t = z.shape[0]
    rows = min(512, seq)
    per_seq = seq // rows
    w, lvl = _hgrn_tables()
    col = lambda c0: pl.BlockSpec((rows, C_DIM), lambda b, h, c: (b * per_seq + c, c0 // C_DIM + h))
    const = lambda shape: pl.BlockSpec(shape, lambda b, h, c: (0,) * len(shape))
    return pl.pallas_call(
        functools.partial(_hgrn_kernel, layer=layer),
        grid=(batch, C_HEADS, per_seq),
        in_specs=[
            pl.BlockSpec((lb_logits.shape[0], C_DIM), lambda b, h, c: (0, h)),
            pl.BlockSpec((1, C_DIM), lambda b, h, c: (0, h)),
            const(w.shape),
            const(lvl.shape),
            col(COL_QC), col(COL_FC), col(COL_IC), col(COL_GC),
        ],
        out_specs=pl.BlockSpec((rows, C_DIM), lambda b, h, c: (b * per_seq + c, h)),
        out_shape=jax.ShapeDtypeStruct((t, C_WIDTH), BF16),
        scratch_shapes=[pltpu.VMEM((C_DIM, C_DIM), F32)],
        compiler_params=_cparams("parallel", "parallel", "arbitrary"),
        name="hgrn2",
    )(lb_logits, norm_g, jnp.asarray(w, BF16), jnp.asarray(lvl), z, z, z, z)


def _merge_kernel(x_ref, oa_ref, ob_ref, oc_ref, d0_ref, d1_ref, d2_ref, l0_ref, l1_ref, l2_ref, gt_ref,
                  wa_ref, wb_ref, wc_ref, wd_ref, wo_ref, g_ref, b_ref, o_ref, obf_ref, *, alpha):
    d = x_ref.shape[1]
    l0, l1, l2 = l0_ref[...], l1_ref[...], l2_ref[...]
    m = jnp.maximum(jnp.maximum(l0, l1), l2)
    e0, e1, e2 = jnp.exp(l0 - m), jnp.exp(l1 - m), jnp.exp(l2 - m)
    inv = 1.0 / (e0 + e1 + e2)
    od = (e0 * inv) * d0_ref[...] + (e1 * inv) * d1_ref[...] + (e2 * inv) * d2_ref[...]
    merged = gt_ref[:, 0:d].astype(F32) * _dot(oa_ref[...], wa_ref[...])
    merged = merged + gt_ref[:, d:2 * d].astype(F32) * _dot(ob_ref[...], wb_ref[...])
    merged = merged + gt_ref[:, 2 * d:3 * d].astype(F32) * _dot(oc_ref[...], wc_ref[...])
    merged = merged + gt_ref[:, 3 * d:4 * d].astype(F32) * _dot(od.astype(BF16), wd_ref[...])
    y = alpha * x_ref[...] + _dot(merged.astype(BF16), wo_ref[...])
    o = _layer_norm(y, g_ref[...], b_ref[...])
    o_ref[...] = o
    obf_ref[...] = o.astype(BF16)


def _merge(x, oa, ob, oc, ods, lses, gates, wa, wb, wc, wd, wo, layer, ln_g, ln_b, alpha):
    t, d = x.shape
    tm = min(256, t)
    row = lambda width: pl.BlockSpec((tm, width), lambda i: (i, 0))
    wspec = lambda k: pl.BlockSpec((None, k, d), lambda i: (layer, 0, 0), pipeline_mode=pl.Buffered(1))
    vec = pl.BlockSpec((1, d), lambda i: (0, 0))
    return pl.pallas_call(
        functools.partial(_merge_kernel, alpha=alpha),
        grid=(t // tm,),
        in_specs=[row(d), row(A_WIDTH), row(B_WIDTH), row(C_WIDTH)] + [row(D_GROUP_WIDTH)] * 6
        + [row(N_BRANCHES * d), wspec(A_WIDTH), wspec(B_WIDTH), wspec(C_WIDTH), wspec(D_GROUP_WIDTH), wspec(d),
           vec, vec],
        out_specs=[row(d), row(d)],
        out_shape=[jax.ShapeDtypeStruct((t, d), F32), jax.ShapeDtypeStruct((t, d), BF16)],
        compiler_params=_cparams("parallel"),
        name="merge_out_ln",
    )(x, oa, ob, oc, *ods, *lses, gates, wa, wb, wc, wd, wo, ln_g, ln_b)


def _ple_kernel(xb_ref, p_ref, wg_ref, wp_ref, o_ref):
    gate = jax.nn.sigmoid(_dot(xb_ref[...], wg_ref[...]))
    o_ref[...] = gate * _dot(p_ref[...].astype(BF16), wp_ref[...])


def _ple(xb, p, wg, wp, layer):
    t, d = xb.shape
    pd = p.shape[-1]
    tm = min(1024, t)
    tn = min(1024, d)
    return pl.pallas_call(
        _ple_kernel,
        grid=(d // tn, t // tm),
        in_specs=[
            pl.BlockSpec((tm, d), lambda n, m: (m, 0)),
            pl.BlockSpec((None, tm, pd), lambda n, m: (layer, m, 0)),
            pl.BlockSpec((None, d, tn), lambda n, m: (layer, 0, n)),
            pl.BlockSpec((None, pd, tn), lambda n, m: (layer, 0, n)),
        ],
        out_specs=pl.BlockSpec((tm, tn), lambda n, m: (m, n)),
        out_shape=jax.ShapeDtypeStruct((t, d), F32),
        compiler_params=_cparams("parallel", "arbitrary"),
        name="ple_gate",
    )(xb, p, wg, wp)


def _alibi_slopes(n):
    return jnp.exp2(-8.0 * jnp.arange(1, n + 1, dtype=F32) / n)


def kernel(x, p, ln_g, ln_b, ffn_w_gate, ffn_w_up, ffn_w_down, w_in, attn_sinks, gmlp_ln_g, gmlp_ln_b, gmlp_w_s,
           gmlp_b_s, hgrn_lb_logits, hgrn_norm_g, w_br_a, w_br_b, w_br_c, w_br_d, w_out, ple_w_proj, ple_w_gate):
    batch, seq, d = x.shape
    depth = ln_g.shape[0]
    t = batch * seq
    alpha = (2.0 * depth) ** 0.25
    assert seq % (BLOCK * D_PATTERNS[-1][1]) == 0, "the widest dilation needs whole 128-row residue blocks"
    assert w_in.shape[-1] == MIX_WIDTH + N_BRANCHES * d
    bf =lambda a: a.astype(BF16)
    wg, wu, wd = bf(ffn_w_gate), bf(ffn_w_up), bf(ffn_w_down)
    w_in_b = bf(w_in)
    wa, wb, wc, wdd, wo = bf(w_br_a), bf(w_br_b), bf(w_br_c), bf(w_br_d), bf(w_out)
    wpp, wpg = bf(ple_w_proj), bf(ple_w_gate)
    p2 = p.reshape(depth, t, p.shape[-1])
    slopes = _alibi_slopes(N_SOFTMAX_HEADS)
    no_sinks = jnp.zeros((D_HEADS,), F32)

    xf = x.reshape(t, d)
    for i in range(depth):
        xf, xb = _ffn(xf, wg, wu, wd, i, 0, ln_g[i, 0][None], ln_b[i, 0][None], None, alpha)
        z = _proj(xb, w_in_b, i, 0, MIX_WIDTH, F32, False)
        gates = _proj(xb, w_in_b, i, MIX_WIDTH, N_BRANCHES * d, BF16, True)
        oa = _band_attn(z, slopes[:A_Q_HEADS], attn_sinks[i], seq=seq, q_col=COL_QA, k_col=COL_KA, v_col=COL_VA,
                        n_pairs=A_Q_HEADS // 2, dil=1, max_dist=A_WINDOW - 1, gqa=True, use_sinks=True,
                        want_lse=False, out_dtype=BF16)[0]
        ob = _gmlp(z, gmlp_ln_g[i][None], gmlp_ln_b[i][None], gmlp_w_s[i], gmlp_b_s[i].T)
        oc = _hgrn(z, hgrn_lb_logits, hgrn_norm_g[i][None], i, batch, seq)
        ods, lses = [], []
        for g, (window, dil) in enumerate(D_PATTERNS):
            h0 = A_Q_HEADS + g * D_HEADS
            od, lse = _band_attn(z, slopes[h0:h0 + D_HEADS], no_sinks, seq=seq, q_col=COL_QD + g * D_GROUP_WIDTH,
                                 k_col=COL_KD + g * D_GROUP_WIDTH, v_col=COL_VD + g * D_GROUP_WIDTH,
                                 n_pairs=D_HEADS // 2, dil=dil, max_dist=window // dil, gqa=False, use_sinks=False,
                                 want_lse=True, out_dtype=F32)
            ods.append(od)
            lses.append(lse)
        xf, xb = _merge(xf, oa, ob, oc, ods, lses, gates, wa, wb, wc, wdd, wo, i, ln_g[i, 1][None], ln_b[i, 1][None],
                        alpha)
        ple = _ple(xb, p2, wpg, wpp, i)
        xf, xb = _ffn(xf, wg, wu, wd, i, 1, ln_g[i, 2][None], ln_b[i, 2][None], ple, alpha)
    return xf.reshape(batch, seq, d)
```

```python
import functools

import numpy as np

import jax
import jax.numpy as jnp
from jax import lax
from jax.experimental import pallas as pl
from jax.experimental.pallas import tpu as pltpu

F32 = jnp.float32
BF16 = jnp.bfloat16

LANES = 128
VMEM_LIMIT_BYTES = 56 * 1024 * 1024

HEAD_DIM = 64
BLOCK = 128
LN_EPS = 1e-5
NEG_BIG = -1e30
F_MIN = 1e-6
A_Q_HEADS = 8
A_KV_HEADS = 2
A_WINDOW = 128
B_GROUPS = 4
B_CHUNK = 128
C_HEADS = 4
C_DIM = 128
D_PATTERNS = ((128, 1), (512, 4), (2048, 16))
D_HEADS = 4
N_SOFTMAX_HEADS = A_Q_HEADS + len(D_PATTERNS) * D_HEADS
N_BRANCHES = 4
A_WIDTH = A_Q_HEADS * HEAD_DIM
A_KV_WIDTH = A_KV_HEADS * HEAD_DIM
B_WIDTH = B_GROUPS * B_CHUNK
C_WIDTH = C_HEADS * C_DIM
D_GROUP_WIDTH = D_HEADS * HEAD_DIM
D_WIDTH = len(D_PATTERNS) * D_GROUP_WIDTH
COL_QA = 0
COL_KA = COL_QA + A_WIDTH
COL_VA = COL_KA + A_KV_WIDTH
COL_UB = COL_VA + A_KV_WIDTH
COL_VB = COL_UB + B_WIDTH
COL_QC = COL_VB + B_WIDTH
COL_FC = COL_QC + C_WIDTH
COL_IC = COL_FC + C_WIDTH
COL_GC = COL_IC + C_WIDTH
COL_QD = COL_GC + C_WIDTH
COL_KD = COL_QD + D_WIDTH
COL_VD = COL_KD + D_WIDTH
MIX_WIDTH = COL_VD + D_WIDTH

ATTN_UNROLL = 4
HGRN_CHUNK = 128
HGRN_LEVELS = (64, 32, 16, 8, 4, 2, 1)


def _cparams(*sem):
    return pltpu.CompilerParams(dimension_semantics=sem, vmem_limit_bytes=VMEM_LIMIT_BYTES)


def _layer_norm(v, g, b):
    mu = jnp.mean(v, axis=-1, keepdims=True)
    d = v - mu
    var = jnp.mean(d * d, axis=-1, keepdims=True)
    return d * lax.rsqrt(var + LN_EPS) * g + b


def _dot(a, b):
    return jnp.dot(a, b, preferred_element_type=F32)


def _dot_nt(a, b):
    return lax.dot_general(a, b, (((1,), (1,)), ((), ())), preferred_element_type=F32)


def _dot_tn(a, b):
    return lax.dot_general(a, b, (((0,), (0,)), ((), ())), preferred_element_type=F32)


def _ffn_kernel(x_ref, wg_ref, wu_ref, wd_ref, g_ref, b_ref, *rest, alpha, has_extra):
    if has_extra:
        e_ref, o_ref, ob_ref, xb_ref, acc_ref = rest
    else:
        o_ref, ob_ref, xb_ref, acc_ref = rest
    f = pl.program_id(1)

    @pl.when(f == 0)
    def _():
        xb_ref[...] = x_ref[...].astype(BF16)
        acc_ref[...] = jnp.zeros_like(acc_ref)

    xb = xb_ref[...]
    g = _dot(xb, wg_ref[...])
    u = _dot(xb, wu_ref[...])
    h = (g * jax.nn.sigmoid(g) * u).astype(BF16)
    acc_ref[...] += _dot(h, wd_ref[...])

    @pl.when(f == pl.num_programs(1) - 1)
    def _():
        y = alpha * x_ref[...] + 0.5 * acc_ref[...]
        if has_extra:
            y = y + e_ref[...]
        o = _layer_norm(y, g_ref[...], b_ref[...])
        o_ref[...] = o
        ob_ref[...] = o.astype(BF16)


def _ffn(x, wg, wu, wd, layer, which, ln_g, ln_b, extra, alpha):
    t, d = x.shape
    ff = wg.shape[-1]
    tm = min(512, t)
    tf = min(512, ff)
    row = pl.BlockSpec((tm, d), lambda m, f: (m, 0))
    vec = pl.BlockSpec((1, d), lambda m, f: (0, 0))
    in_specs = [
        row,
        pl.BlockSpec((None, None, d, tf), lambda m, f: (layer, which, 0, f)),
        pl.BlockSpec((None, None, d, tf), lambda m, f: (layer, which, 0, f)),
        pl.BlockSpec((None, None, tf, d), lambda m, f: (layer, which, f, 0)),
        vec,
        vec,
    ]
    args = [x, wg, wu, wd, ln_g, ln_b]
    if extra is not None:
        in_specs.append(row)
        args.append(extra)
    return pl.pallas_call(
        functools.partial(_ffn_kernel, alpha=alpha, has_extra=extra is not None),
        grid=(t // tm, ff // tf),
        in_specs=in_specs,
        out_specs=[row, row],
        out_shape=[jax.ShapeDtypeStruct((t, d), F32), jax.ShapeDtypeStruct((t, d), BF16)],
        scratch_shapes=[pltpu.VMEM((tm, d), BF16), pltpu.VMEM((tm, d), F32)],
        compiler_params=_cparams("parallel", "arbitrary"),
        name="ffn_ln",
    )(*args)


def _proj_kernel(x_ref, w_ref, o_ref, *, act):
    z = _dot(x_ref[...], w_ref[...])
    if act:
        z = jax.nn.sigmoid(z)
    o_ref[...] = z.astype(o_ref.dtype)


def _proj(xb, w, layer, col0, width, out_dtype, act):
    t, d = xb.shape
    tm = min(2048, t)
    tn = min(1024, width)
    off = col0 // tn
    return pl.pallas_call(
        functools.partial(_proj_kernel, act=act),
        grid=(width // tn, t // tm),
        in_specs=[
            pl.BlockSpec((tm, d), lambda n, m: (m, 0)),
            pl.BlockSpec((None, d, tn), lambda n, m: (layer, 0, n + off)),
        ],
        out_specs=pl.BlockSpec((tm, tn), lambda n, m: (m, n)),
        out_shape=jax.ShapeDtypeStruct((t, width), out_dtype),
        compiler_params=_cparams("parallel", "arbitrary"),
        name="in_proj_gates" if act else "in_proj_mix",
    )(xb, w)


def _band_attn_kernel(slope_ref, sink_ref, q_ref, k_ref, v_ref, kp_ref, vp_ref, *rest,
                      dil, nblk, max_dist, sb_per_seq, gqa, use_sinks, want_lse):
    if want_lse:
        o_ref, lse_ref, kbuf, vbuf = rest
    else:
        o_ref, kbuf, vbuf = rest
        lse_ref = None
    i = pl.program_id(0)
    p = pl.program_id(1)
    bw = BLOCK * dil
    first_in_seq = (i % sb_per_seq) == 0
    lane = lax.broadcasted_iota(jnp.int32, (1, LANES), 1)
    low = lane < HEAD_DIM

    def stage(cur_ref, prev_ref, buf):
        cur = cur_ref[...]
        prev = prev_ref[...]
        if gqa:
            kv_first = (p // 2) == 0

            def dup(a):
                r = pltpu.roll(a, HEAD_DIM, 1)
                return jnp.where(kv_first, jnp.where(low, a, r), jnp.where(low, r, a))

            cur, prev = dup(cur), dup(prev)
        buf[0:bw, :] = prev
        buf[bw:, :] = cur

    if gqa:
        @pl.when(p % 2 == 0)
        def _():
            stage(k_ref, kp_ref, kbuf)
            stage(v_ref, vp_ref, vbuf)
    else:
        stage(k_ref, kp_ref, kbuf)
        stage(v_ref, vp_ref, vbuf)

    q_off = lax.broadcasted_iota(jnp.int32, (2 * BLOCK, 2 * BLOCK), 0) % BLOCK
    k_off = lax.broadcasted_iota(jnp.int32, (2 * BLOCK, 2 * BLOCK), 1)
    dist = q_off + BLOCK - k_off
    in_band = (dist >= 0) & (dist <= max_dist)
    is_cur = k_off >= BLOCK
    dist_f = (jnp.maximum(dist, 0) * dil).astype(F32)
    top = lax.broadcasted_iota(jnp.int32, (2 * BLOCK, 1), 0) < BLOCK
    neg_slope = jnp.where(top, -slope_ref[2 * p], -slope_ref[2 * p + 1])
    bias = neg_slope * dist_f
    if use_sinks:
        sink = jnp.where(top, sink_ref[2 * p], sink_ref[2 * p + 1])

    def body(it, carry):
        j = it // dil
        c = it % dil
        row0 = j * bw + c
        if dil == 1:
            row0 = pl.multiple_of(row0, BLOCK)
            qs = pl.ds(row0, BLOCK)
            ks = pl.ds(row0, 2 * BLOCK)
        else:
            qs = pl.ds(row0, BLOCK, stride=dil)
            ks = pl.ds(row0, 2 * BLOCK, stride=dil)
        qc = q_ref[qs, :] * (HEAD_DIM ** -0.5)
        kk = kbuf[ks, :]
        vv = vbuf[ks, :]
        q2 = jnp.concatenate([jnp.where(low, qc, 0.0), jnp.where(low, 0.0, qc)], axis=0).astype(BF16)
        s = _dot_nt(q2, kk.astype(BF16)) + bias
        has_prev = jnp.logical_or(j > 0, jnp.logical_not(first_in_seq))
        valid = in_band & (is_cur | has_prev)
        s = jnp.where(valid, s, NEG_BIG)
        m = jnp.max(s, axis=-1, keepdims=True)
        if use_sinks:
            m = jnp.maximum(m, sink)
        pr = jnp.exp(s - m)
        den = jnp.sum(pr, axis=-1, keepdims=True)
        if use_sinks:
            den = den + jnp.exp(sink - m)
        prb = pr.astype(BF16)
        o0 = _dot(prb[0:BLOCK], jnp.where(low, vv, 0.0).astype(BF16))
        o1 = _dot(prb[BLOCK:], jnp.where(low, 0.0, vv).astype(BF16))
        inv = 1.0 / den
        o = o0 * inv[0:BLOCK] + o1 * inv[BLOCK:]
        o_ref[qs, :] = o.astype(o_ref.dtype)
        if want_lse:
            lse = m + jnp.log(den)
            lse_ref[qs, :] = jnp.where(low, lse[0:BLOCK], lse[BLOCK:])
        return carry

    lax.fori_loop(0, nblk * dil, body, 0, unroll=ATTN_UNROLL)


def _band_attn(z, slopes, sinks, *, seq, q_col, k_col, v_col, n_pairs, dil, max_dist, gqa, use_sinks,
               want_lse, out_dtype):
    t = z.shape[0]
    bw = BLOCK * dil
    sb = min(max(bw, 1024), seq)
    nblk = sb // bw
    qb, kb, vb = q_col // LANES, k_col // LANES, v_col // LANES
    if gqa:
        kv_map = lambda i, p: (i, kb)
        vv_map = lambda i, p: (i, vb)
        kp_map = lambda i, p: (jnp.maximum(i * nblk - 1, 0), kb)
        vp_map = lambda i, p: (jnp.maximum(i * nblk - 1, 0), vb)
    else:
        kv_map = lambda i, p: (i, kb + p)
        vv_map = lambda i, p: (i, vb + p)
        kp_map = lambda i, p: (jnp.maximum(i * nblk - 1, 0), kb + p)
        vp_map = lambda i, p: (jnp.maximum(i * nblk - 1, 0), vb + p)
    smem = pl.BlockSpec(memory_space=pltpu.SMEM)
    out_spec = pl.BlockSpec((sb, LANES), lambda i, p: (i, p))
    out_specs = [out_spec]
    out_shape = [jax.ShapeDtypeStruct((t, n_pairs * LANES), out_dtype)]
    if want_lse:
        out_specs.append(out_spec)
        out_shape.append(jax.ShapeDtypeStruct((t, n_pairs * LANES), F32))
    return pl.pallas_call(
        functools.partial(_band_attn_kernel, dil=dil, nblk=nblk, max_dist=max_dist, sb_per_seq=seq // sb,
                          gqa=gqa, use_sinks=use_sinks, want_lse=want_lse),
        grid=(t // sb, n_pairs),
        in_specs=[
            smem,
            smem,
            pl.BlockSpec((sb, LANES), lambda i, p: (i, qb + p)),
            pl.BlockSpec((sb, LANES), kv_map),
            pl.BlockSpec((sb, LANES), vv_map),
            pl.BlockSpec((bw, LANES), kp_map),
            pl.BlockSpec((bw, LANES), vp_map),
        ],
        out_specs=out_specs,
        out_shape=out_shape,
        scratch_shapes=[pltpu.VMEM((bw + sb, LANES), F32), pltpu.VMEM((bw + sb, LANES), F32)],
        compiler_params=_cparams("parallel", "arbitrary"),
        name=f"band_attn_d{dil}" + ("_gqa" if gqa else ""),
    )(slopes, sinks, z, z, z, z, z)


def _gelu(x):
    return 0.5 * x * (1.0 + lax.erf(x * np.float32(np.sqrt(0.5))))


def _gmlp_kernel(u0_ref, u1_ref, v0_ref, v1_ref, g_ref, b_ref, ws_ref, bs_ref, o_ref):
    rows = u0_ref.shape[0]
    u = _gelu(jnp.concatenate([u0_ref[...], u1_ref[...]], axis=1))
    v = _gelu(jnp.concatenate([v0_ref[...], v1_ref[...]], axis=1))
    v = _layer_norm(v, g_ref[...], b_ref[...]).astype(BF16)
    t_idx = lax.broadcasted_iota(jnp.int32, (B_CHUNK, B_CHUNK), 0)
    s_idx = lax.broadcasted_iota(jnp.int32, (B_CHUNK, B_CHUNK), 1)
    causal = t_idx >= s_idx
    for grp in range(B_GROUPS):
        w = jnp.where(causal, ws_ref[grp], 0.0).astype(BF16)
        bias = bs_ref[:, grp:grp + 1]
        cols = slice(grp * B_CHUNK, (grp + 1) * B_CHUNK)
        for ch in range(rows // B_CHUNK):
            rs = slice(ch * B_CHUNK, (ch + 1) * B_CHUNK)
            mixed = _dot(w, v[rs, cols]) + bias
            o_ref[rs, cols] = (u[rs, cols] * mixed).astype(o_ref.dtype)


def _gmlp(z, ln_g, ln_b, w_s, b_s_t):
    t = z.shape[0]
    rows = min(512, t)
    half = B_WIDTH // 2
    ub, vb = COL_UB // half, COL_VB // half
    col = lambda j: pl.BlockSpec((rows, half), lambda i: (i, j))
    vec = pl.BlockSpec((1, B_WIDTH), lambda i: (0, 0))
    return pl.pallas_call(
        _gmlp_kernel,
        grid=(t // rows,),
        in_specs=[col(ub), col(ub + 1), col(vb), col(vb + 1), vec, vec,
                  pl.BlockSpec((B_GROUPS, B_CHUNK, B_CHUNK), lambda i: (0, 0, 0)),
                  pl.BlockSpec((B_CHUNK, B_GROUPS), lambda i: (0, 0))],
        out_specs=pl.BlockSpec((rows, B_WIDTH), lambda i: (i, 0)),
        out_shape=jax.ShapeDtypeStruct((t, B_WIDTH), BF16),
        compiler_params=_cparams("parallel"),
        name="spatial_gating",
    )(z, z, z, z, ln_g, ln_b, w_s, b_s_t)


def _hgrn_tables():
    n = HGRN_CHUNK
    t = np.arange(n)[:, None]
    i = np.arange(n)[None, :]
    tabs = []
    for h in HGRN_LEVELS:
        start = (t // h) * h
        upper = ((t // h) % 2) == 1
        q_tab = (i >= start) & (i <= t)
        k_tab = (i > t) & (i < start + h)
        tabs.append(np.where(upper, q_tab, k_tab))
    tabs.append(i <= t)
    tabs.append(i > t)
    w = np.concatenate(tabs, axis=0).astype(np.float32)
    s = np.arange(n)[None, :]
    x = np.bitwise_xor(t, s)
    level = np.where(s < t, np.floor(np.log2(np.maximum(x, 1))).astype(np.int32), np.where(s == t, -1, -2))
    return w, level.astype(np.int32)


def _hgrn_kernel(lbl_ref, ng_ref, w_ref, lvl_ref, q0_ref, q1_ref, f0_ref, f1_ref, i0_ref, i1_ref, g0_ref, g1_ref,
                 o_ref, st_ref, *, layer):
    n = HGRN_CHUNK

    @pl.when(pl.program_id(1) == 0)
    def _():
        st_ref[...] = jnp.zeros_like(st_ref)

    logits = lbl_ref[...]
    e = jnp.exp(logits - jnp.max(logits, axis=0, keepdims=True))
    probs = e / jnp.sum(e, axis=0, keepdims=True)
    lb = jnp.zeros((1, C_WIDTH), F32)
    for j in range(1, layer + 1):
        lb = lb + probs[j:j + 1, :]

    w = w_ref[...]
    lvl = lvl_ref[...]
    row = lax.broadcasted_iota(jnp.int32, (n, C_WIDTH), 0)
    ng = ng_ref[...]
    nl = len(HGRN_LEVELS)
    for ch in range(q0_ref.shape[0] // n):
        rs = slice(ch * n, (ch + 1) * n)
        both = lambda a_ref, b_ref: jnp.concatenate([a_ref[rs, :], b_ref[rs, :]], axis=1)
        z = both(f0_ref, f1_ref)
        q = both(q0_ref, q1_ref)
        f = lb + (1.0 - lb) * jax.nn.sigmoid(z)
        logf = jnp.log(jnp.maximum(f, F_MIN))
        kk = (1.0 - lb) * jax.nn.sigmoid(-z)
        vb = both(i0_ref, i1_ref).astype(BF16)
        l1 = logf.astype(BF16)
        l2 = (logf - l1.astype(F32)).astype(BF16)
        expo = _dot(w, l1) + _dot(w, l2)
        xs = []
        for li, h in enumerate(HGRN_LEVELS):
            upper = (row & h) != 0
            xs.append((jnp.where(upper, q, kk) * jnp.exp(expo[li * n:(li + 1) * n])).astype(BF16))
        q_in = (q * jnp.exp(expo[nl * n:(nl + 1) * n])).astype(BF16)
        k_out = (kk * jnp.exp(expo[(nl + 1) * n:(nl + 2) * n])).astype(BF16)
        carry = jnp.exp(expo[(nl + 1) * n - 1:(nl + 1) * n])
        qb, kb = q.astype(BF16), kk.astype(BF16)
        outs = []
        for hd in range(C_HEADS):
            cs = slice(hd * C_DIM, (hd + 1) * C_DIM)
            scores = jnp.where(lvl == -1, _dot_nt(qb[:, cs], kb[:, cs]), 0.0)
            for li, h in enumerate(HGRN_LEVELS):
                scores = scores + jnp.where(lvl == int(np.log2(h)), _dot_nt(xs[li][:, cs], xs[li][:, cs]), 0.0)
            st = st_ref[hd]
            o = _dot(scores.astype(BF16), vb[:, cs]) + _dot_nt(q_in[:, cs], st.astype(BF16))
            st_ref[hd] = st * carry[:, cs] + _dot_tn(vb[:, cs], k_out[:, cs])
            outs.append(o * lax.rsqrt(jnp.mean(o * o, axis=-1, keepdims=True) + LN_EPS))
        o = jnp.concatenate(outs, axis=1) * ng
        o_ref[rs, :] = (o * jax.nn.sigmoid(both(g0_ref, g1_ref))).astype(o_ref.dtype)


def _hgrn(z, lb_logits, norm_g, layer, batch, seq):
    t = z.shape[0]
    rows = min(512, seq)
    per_seq = seq // rows
    half = C_WIDTH // 2
    w, lvl = _hgrn_tables()
    col = lambda c0, j: pl.BlockSpec((rows, half), lambda b, c: (b * per_seq + c, c0 // half + j))
    const = lambda shape: pl.BlockSpec(shape, lambda b, c: (0,) * len(shape))
    return pl.pallas_call(
        functools.partial(_hgrn_kernel, layer=layer),
        grid=(batch, per_seq),
        in_specs=[const(lb_logits.shape), const(norm_g.shape), const(w.shape), const(lvl.shape)]
        + [col(c0, j) for c0 in (COL_QC, COL_FC, COL_IC, COL_GC) for j in (0, 1)],
        out_specs=pl.BlockSpec((rows, C_WIDTH), lambda b, c: (b * per_seq + c, 0)),
        out_shape=jax.ShapeDtypeStruct((t, C_WIDTH), BF16),
        scratch_shapes=[pltpu.VMEM((C_HEADS, C_DIM, C_DIM), F32)],
        compiler_params=_cparams("parallel", "arbitrary"),
        name="hgrn2",
    )(lb_logits, norm_g, jnp.asarray(w, BF16), jnp.asarray(lvl), *([z] * 8))


def _merge_kernel(x_ref, oa_ref, ob_ref, oc_ref, d0_ref, d1_ref, d2_ref, l0_ref, l1_ref, l2_ref, gt_ref,
                  wa_ref, wb_ref, wc_ref, wd_ref, wo_ref, g_ref, b_ref, o_ref, obf_ref, *, alpha):
    d = x_ref.shape[1]
    l0, l1, l2 = l0_ref[...], l1_ref[...], l2_ref[...]
    m = jnp.maximum(jnp.maximum(l0, l1), l2)
    e0, e1, e2 = jnp.exp(l0 - m), jnp.exp(l1 - m), jnp.exp(l2 - m)
    inv = 1.0 / (e0 + e1 + e2)
    od = (e0 * inv) * d0_ref[...] + (e1 * inv) * d1_ref[...] + (e2 * inv) * d2_ref[...]
    merged = gt_ref[:, 0:d].astype(F32) * _dot(oa_ref[...], wa_ref[...])
    merged = merged + gt_ref[:, d:2 * d].astype(F32) * _dot(ob_ref[...], wb_ref[...])
    merged = merged + gt_ref[:, 2 * d:3 * d].astype(F32) * _dot(oc_ref[...], wc_ref[...])
    merged = merged + gt_ref[:, 3 * d:4 * d].astype(F32) * _dot(od.astype(BF16), wd_ref[...])
    y = alpha * x_ref[...] + _dot(merged.astype(BF16), wo_ref[...])
    o = _layer_norm(y, g_ref[...], b_ref[...])
    o_ref[...] = o
    obf_ref[...] = o.astype(BF16)


def _merge(x, oa, ob, oc, ods, lses, gates, wa, wb, wc, wd, wo, layer, ln_g, ln_b, alpha):
    t, d = x.shape
    tm = min(256, t)
    row = lambda width: pl.BlockSpec((tm, width), lambda i: (i, 0))
    wspec = lambda k: pl.BlockSpec((None, k, d), lambda i: (layer, 0, 0), pipeline_mode=pl.Buffered(1))
    vec = pl.BlockSpec((1, d), lambda i: (0, 0))
    return pl.pallas_call(
        functools.partial(_merge_kernel, alpha=alpha),
        grid=(t // tm,),
        in_specs=[row(d), row(A_WIDTH), row(B_WIDTH), row(C_WIDTH)] + [row(D_GROUP_WIDTH)] * 6
        + [row(N_BRANCHES * d), wspec(A_WIDTH), wspec(B_WIDTH), wspec(C_WIDTH), wspec(D_GROUP_WIDTH), wspec(d),
           vec, vec],
        out_specs=[row(d), row(d)],
        out_shape=[jax.ShapeDtypeStruct((t, d), F32), jax.ShapeDtypeStruct((t, d), BF16)],
        compiler_params=_cparams("parallel"),
        name="merge_out_ln",
    )(x, oa, ob, oc, *ods, *lses, gates, wa, wb, wc, wd, wo, ln_g, ln_b)


def _ple_kernel(xb_ref, p_ref, wg_ref, wp_ref, o_ref):
    gate = jax.nn.sigmoid(_dot(xb_ref[...], wg_ref[...]))
    o_ref[...] = gate * _dot(p_ref[...].astype(BF16), wp_ref[...])


def _ple(xb, p, wg, wp, layer):
    t, d = xb.shape
    pd = p.shape[-1]
    tm = min(1024, t)
    tn = min(1024, d)
    return pl.pallas_call(
        _ple_kernel,
        grid=(d // tn, t // tm),
        in_specs=[
            pl.BlockSpec((tm, d), lambda n, m: (m, 0)),
            pl.BlockSpec((None, tm, pd), lambda n, m: (layer, m, 0)),
            pl.BlockSpec((None, d, tn), lambda n, m: (layer, 0, n)),
            pl.BlockSpec((None, pd, tn), lambda n, m: (layer, 0, n)),
        ],
        out_specs=pl.BlockSpec((tm, tn), lambda n, m: (m, n)),
        out_shape=jax.ShapeDtypeStruct((t, d), F32),
        compiler_params=_cparams("parallel", "arbitrary"),
        name="ple_gate",
    )(xb, p, wg, wp)


def _alibi_slopes(n):
    return jnp.exp2(-8.0 * jnp.arange(1, n + 1, dtype=F32) / n)


def kernel(x, p, ln_g, ln_b, ffn_w_gate, ffn_w_up, ffn_w_down, w_in, attn_sinks, gmlp_ln_g, gmlp_ln_b, gmlp_w_s,
           gmlp_b_s, hgrn_lb_logits, hgrn_norm_g, w_br_a, w_br_b, w_br_c, w_br_d, w_out, ple_w_proj, ple_w_gate):
    batch, seq, d = x.shape
    depth = ln_g.shape[0]
    t = batch * seq
    alpha = (2.0 * depth) ** 0.25
    assert seq % (BLOCK * D_PATTERNS[-1][1]) == 0, "the widest dilation needs whole 128-row residue blocks"
    assert w_in.shape[-1] == MIX_WIDTH + N_BRANCHES * d
    bf =lambda a: a.astype(BF16)
    wg, wu, wd = bf(ffn_w_gate), bf(ffn_w_up), bf(ffn_w_down)
    w_in_b = bf(w_in)
    wa, wb, wc, wdd, wo = bf(w_br_a), bf(w_br_b), bf(w_br_c), bf(w_br_d), bf(w_out)
    wpp, wpg = bf(ple_w_proj), bf(ple_w_gate)
    p2 = p.reshape(depth, t, p.shape[-1])
    slopes = _alibi_slopes(N_SOFTMAX_HEADS)
    no_sinks = jnp.zeros((D_HEADS,), F32)

    xf = x.reshape(t, d)
    for i in range(depth):
        xf, xb = _ffn(xf, wg, wu, wd, i, 0, ln_g[i, 0][None], ln_b[i, 0][None], None, alpha)
        z = _proj(xb, w_in_b, i, 0, MIX_WIDTH, F32, False)
        gates = _proj(xb, w_in_b, i, MIX_WIDTH, N_BRANCHES * d, BF16, True)
        oa = _band_attn(z, slopes[:A_Q_HEADS], attn_sinks[i], seq=seq, q_col=COL_QA, k_col=COL_KA, v_col=COL_VA,
                        n_pairs=A_Q_HEADS // 2, dil=1, max_dist=A_WINDOW - 1, gqa=True, use_sinks=True,
                        want_lse=False, out_dtype=BF16)[0]
        ob = _gmlp(z, gmlp_ln_g[i][None], gmlp_ln_b[i][None], gmlp_w_s[i], gmlp_b_s[i].T)
        oc = _hgrn(z, hgrn_lb_logits, hgrn_norm_g[i][None], i, batch, seq)
        ods, lses = [], []
        for g, (window, dil) in enumerate(D_PATTERNS):
            h0 = A_Q_HEADS + g * D_HEADS
            od, lse = _band_attn(z, slopes[h0:h0 + D_HEADS], no_sinks, seq=seq, q_col=COL_QD + g * D_GROUP_WIDTH,
                                 k_col=COL_KD + g * D_GROUP_WIDTH, v_col=COL_VD + g * D_GROUP_WIDTH,
                                 n_pairs=D_HEADS // 2, dil=dil, max_dist=window // dil, gqa=False, use_sinks=False,
                                 want_lse=True, out_dtype=F32)
            ods.append(od)
            lses.append(lse)
        xf, xb = _merge(xf, oa, ob, oc, ods, lses, gates, wa, wb, wc, wdd, wo, i, ln_g[i, 1][None], ln_b[i, 1][None],
                        alpha)
        ple = _ple(xb, p2, wpg, wpp, i)
        xf, xb = _ffn(xf, wg, wu, wd, i, 1, ln_g[i, 2][None], ln_b[i, 2][None], ple, alpha)
    return xf.reshape(batch, seq, d)
```

```python
import functools

import numpy as np

import jax
import jax.numpy as jnp
from jax import lax
from jax.experimental import pallas as pl
from jax.experimental.pallas import tpu as pltpu

F32 = jnp.float32
BF16 = jnp.bfloat16

LANES = 128
VMEM_LIMIT_BYTES = 56 * 1024 * 1024

HEAD_DIM = 64
BLOCK = 128
LN_EPS = 1e-5
NEG_BIG = -1e30
F_MIN = 1e-6
A_Q_HEADS = 8
A_KV_HEADS = 2
A_WINDOW = 128
B_GROUPS = 4
B_CHUNK = 128
C_HEADS = 4
C_DIM = 128
D_PATTERNS = ((128, 1), (512, 4), (2048, 16))
D_HEADS = 4
N_SOFTMAX_HEADS = A_Q_HEADS + len(D_PATTERNS) * D_HEADS
N_BRANCHES = 4
A_WIDTH = A_Q_HEADS * HEAD_DIM
A_KV_WIDTH = A_KV_HEADS * HEAD_DIM
B_WIDTH = B_GROUPS * B_CHUNK
C_WIDTH = C_HEADS * C_DIM
D_GROUP_WIDTH = D_HEADS * HEAD_DIM
D_WIDTH = len(D_PATTERNS) * D_GROUP_WIDTH
COL_QA = 0
COL_KA = COL_QA + A_WIDTH
COL_VA = COL_KA + A_KV_WIDTH
COL_UB = COL_VA + A_KV_WIDTH
COL_VB = COL_UB + B_WIDTH
COL_QC = COL_VB + B_WIDTH
COL_FC = COL_QC + C_WIDTH
COL_IC = COL_FC + C_WIDTH
COL_GC = COL_IC + C_WIDTH
COL_QD = COL_GC + C_WIDTH
COL_KD = COL_QD + D_WIDTH
COL_VD = COL_KD + D_WIDTH
MIX_WIDTH = COL_VD + D_WIDTH

ATTN_UNROLL = 8
HGRN_CHUNK = 128
HGRN_LEVELS = (64, 32, 16, 8, 4, 2)


def _cparams(*sem):
    return pltpu.CompilerParams(dimension_semantics=sem, vmem_limit_bytes=VMEM_LIMIT_BYTES)


def _layer_norm(v, g, b):
    mu = jnp.mean(v, axis=-1, keepdims=True)
    d = v - mu
    var = jnp.mean(d * d, axis=-1, keepdims=True)
    return d * lax.rsqrt(var + LN_EPS) * g + b


def _dot(a, b):
    return jnp.dot(a, b, preferred_element_type=F32)


def _dot_nt(a, b):
    return lax.dot_general(a, b, (((1,), (1,)), ((), ())), preferred_element_type=F32)


def _dot_tn(a, b):
    return lax.dot_general(a, b, (((0,), (0,)), ((), ())), preferred_element_type=F32)


def _ffn_kernel(x_ref, wg_ref, wu_ref, wd_ref, g_ref, b_ref, *rest, alpha, has_extra, nf):
    if has_extra:
        e_ref, o_ref, ob_ref, xb_ref, acc_ref = rest
    else:
        o_ref, ob_ref, xb_ref, acc_ref = rest
    i = pl.program_id(0)
    f = pl.program_id(1)
    live = i < pl.num_programs(0) - 1

    def contribution():
        xb = xb_ref[...]
        g = _dot(xb, wg_ref[...])
        u = _dot(xb, wu_ref[...])
        h = (g * jax.nn.sigmoid(g) * u).astype(BF16)
        return _dot(h, wd_ref[...])

    @pl.when((i == 0) & (f == 0))
    def _():
        acc_ref[...] = jnp.zeros_like(acc_ref)

    @pl.when(f == 0)
    def _():
        y = acc_ref[...]
        if has_extra:
            y = y + e_ref[...]
        o = _layer_norm(y, g_ref[...], b_ref[...])
        o_ref[...] = o
        ob_ref[...] = o.astype(BF16)
        xb_ref[...] = x_ref[...].astype(BF16)
        c = contribution()
        acc_ref[...] = alpha * x_ref[...] + 0.5 * c if nf == 1 else c

    if nf > 2:
        @pl.when((f > 0) & (f < nf - 1) & live)
        def _():
            acc_ref[...] += contribution()

    if nf > 1:
        @pl.when((f == nf - 1) & live)
        def _():
            acc_ref[...] = alpha * x_ref[...] + 0.5 * (acc_ref[...] + contribution())


def _ffn(x, wg, wu, wd, layer, which, ln_g, ln_b, extra, alpha, tm=512, tf=512):
    t, d = x.shape
    ff = wg.shape[-1]
    tm = min(tm, t)
    tf = min(tf, ff)
    nt, nf = t // tm, ff // tf
    cur = pl.BlockSpec((tm, d), lambda i, f: (jnp.minimum(i, nt - 1), 0))
    lag = pl.BlockSpec((tm, d), lambda i, f: (jnp.maximum(i - 1, 0), 0))
    vec = pl.BlockSpec((1, d), lambda i, f: (0, 0))
    chunk = lambda i, f: jnp.where(i < nt, f, nf - 1)
    in_specs = [
        cur,
        pl.BlockSpec((None, None, d, tf), lambda i, f: (layer, which, 0, chunk(i, f))),
        pl.BlockSpec((None, None, d, tf), lambda i, f: (layer, which, 0, chunk(i, f))),
        pl.BlockSpec((None, None, tf, d), lambda i, f: (layer, which, chunk(i, f), 0)),
        vec,
        vec,
    ]
    args = [x, wg, wu, wd, ln_g, ln_b]
    if extra is not None:
        in_specs.append(lag)
        args.append(extra)
    return pl.pallas_call(
        functools.partial(_ffn_kernel, alpha=alpha, has_extra=extra is not None, nf=nf),
        grid=(nt + 1, nf),
        in_specs=in_specs,
        out_specs=[lag, lag],
        out_shape=[jax.ShapeDtypeStruct((t, d), F32), jax.ShapeDtypeStruct((t, d), BF16)],
        scratch_shapes=[pltpu.VMEM((tm, d), BF16), pltpu.VMEM((tm, d), F32)],
        compiler_params=_cparams("arbitrary", "arbitrary"),
        name="ffn_ln",
    )(*args)


def _proj_kernel(x_ref, w_ref, o_ref, *, act):
    z = _dot(x_ref[...], w_ref[...])
    if act:
        z = jax.nn.sigmoid(z)
    o_ref[...] = z.astype(o_ref.dtype)


def _proj(xb, w, layer, col0, width, out_dtype, act):
    t, d = xb.shape
    tm = min(2048, t)
    tn = min(1024, width)
    off = col0 // tn
    return pl.pallas_call(
        functools.partial(_proj_kernel, act=act),
        grid=(width // tn, t // tm),
        in_specs=[
            pl.BlockSpec((tm, d), lambda n, m: (m, 0)),
            pl.BlockSpec((None, d, tn), lambda n, m: (layer, 0, n + off)),
        ],
        out_specs=pl.BlockSpec((tm, tn), lambda n, m: (m, n)),
        out_shape=jax.ShapeDtypeStruct((t, width), out_dtype),
        compiler_params=_cparams("parallel", "arbitrary"),
        name="in_proj_gates" if act else "in_proj_mix",
    )(xb, w)


def _band_attn_kernel(slope_ref, sink_ref, q_ref, k_ref, v_ref, kp_ref, vp_ref, *rest,
                      dil, nblk, max_dist, sb_per_seq, gqa, use_sinks, want_lse):
    if want_lse:
        o_ref, lse_ref, kbuf, vbuf, bias_ref = rest
    else:
        o_ref, kbuf, vbuf, bias_ref = rest
        lse_ref = None
    i = pl.program_id(0)
    p = pl.program_id(1)
    bw = BLOCK * dil
    first_in_seq = (i % sb_per_seq) == 0
    lane = lax.broadcasted_iota(jnp.int32, (1, LANES), 1)
    low = lane < HEAD_DIM

    def stage(cur_ref, prev_ref, buf):
        cur = cur_ref[...]
        prev = prev_ref[...]
        if gqa:
            kv_first = (p // 2) == 0

            def dup(a):
                r = pltpu.roll(a, HEAD_DIM, 1)
                return jnp.where(kv_first, jnp.where(low, a, r), jnp.where(low, r, a))

            cur, prev = dup(cur), dup(prev)
        buf[0:bw, :] = prev.astype(buf.dtype)
        buf[bw:, :] = cur.astype(buf.dtype)

    if gqa:
        @pl.when(p % 2 == 0)
        def _():
            stage(k_ref, kp_ref, kbuf)
            stage(v_ref, vp_ref, vbuf)
    else:
        stage(k_ref, kp_ref, kbuf)
        stage(v_ref, vp_ref, vbuf)

    q_off = lax.broadcasted_iota(jnp.int32, (2 * BLOCK, 2 * BLOCK), 0) % BLOCK
    k_off = lax.broadcasted_iota(jnp.int32, (2 * BLOCK, 2 * BLOCK), 1)
    dist = q_off + BLOCK - k_off
    in_band = (dist >= 0) & (dist <= max_dist)
    is_cur = k_off >= BLOCK
    dist_f = (jnp.maximum(dist, 0) * dil).astype(F32)
    top = lax.broadcasted_iota(jnp.int32, (2 * BLOCK, 1), 0) < BLOCK
    neg_slope = jnp.where(top, -slope_ref[2 * p], -slope_ref[2 * p + 1])
    bias = jnp.where(in_band, neg_slope * dist_f, NEG_BIG)
    bias_ref[0] = bias
    bias_ref[1] = jnp.where(is_cur, bias, NEG_BIG)
    if use_sinks:
        sink = jnp.where(top, sink_ref[2 * p], sink_ref[2 * p + 1])
    ones = jnp.ones((2 * BLOCK, LANES), BF16)

    def body(it, carry):
        j = it // dil
        c = it % dil
        row0 = j * bw + c
        if dil == 1:
            row0 = pl.multiple_of(row0, BLOCK)
            qs = pl.ds(row0, BLOCK)
            ks = pl.ds(row0, 2 * BLOCK)
        else:
            qs = pl.ds(row0, BLOCK, stride=dil)
            ks = pl.ds(row0, 2 * BLOCK, stride=dil)
        qc = q_ref[qs, :] * (HEAD_DIM ** -0.5)
        kk = kbuf[ks, :].astype(BF16)
        vv = vbuf[ks, :].astype(BF16)
        q2 = jnp.concatenate([jnp.where(low, qc, 0.0), jnp.where(low, 0.0, qc)], axis=0).astype(BF16)
        tbl = bias_ref[jnp.logical_and(j == 0, first_in_seq).astype(jnp.int32)]
        s = jnp.where(tbl > 0.5 * NEG_BIG, _dot_nt(q2, kk) + tbl, NEG_BIG)
        m = jnp.max(s, axis=-1, keepdims=True)
        if use_sinks:
            m = jnp.maximum(m, sink)
        prb = jnp.exp(s - m).astype(BF16)
        ov = _dot(prb, jnp.concatenate([vv, ones], axis=1))
        den = ov[:, LANES:]
        if use_sinks:
            den = den + jnp.exp(sink - m)
        on = ov[:, :LANES] / den
        o_ref[qs, :] = jnp.where(low, on[0:BLOCK], on[BLOCK:]).astype(o_ref.dtype)
        if want_lse:
            lse = m + jnp.log(den)
            lse_ref[qs, :] = jnp.where(low, lse[0:BLOCK], lse[BLOCK:])
        return carry

    lax.fori_loop(0, nblk * dil, body, 0, unroll=ATTN_UNROLL)


def _band_attn(z, slopes, sinks, *, seq, q_col, k_col, v_col, n_pairs, dil, max_dist, gqa, use_sinks,
               want_lse, out_dtype):
    t = z.shape[0]
    bw = BLOCK * dil
    sb = min(max(bw, 1024), seq)
    nblk = sb // bw
    qb, kb, vb = q_col // LANES, k_col // LANES, v_col // LANES
    if gqa:
        kv_map = lambda i, p: (i, kb)
        vv_map = lambda i, p: (i, vb)
        kp_map = lambda i, p: (jnp.maximum(i * nblk - 1, 0), kb)
        vp_map = lambda i, p: (jnp.maximum(i * nblk - 1, 0), vb)
    else:
        kv_map = lambda i, p: (i, kb + p)
        vv_map = lambda i, p: (i, vb + p)
        kp_map = lambda i, p: (jnp.maximum(i * nblk - 1, 0), kb + p)
        vp_map = lambda i, p: (jnp.maximum(i * nblk - 1, 0), vb + p)
    smem = pl.BlockSpec(memory_space=pltpu.SMEM)
    out_spec = pl.BlockSpec((sb, LANES), lambda i, p: (i, p))
    out_specs = [out_spec]
    out_shape = [jax.ShapeDtypeStruct((t, n_pairs * LANES), out_dtype)]
    if want_lse:
        out_specs.append(out_spec)
        out_shape.append(jax.ShapeDtypeStruct((t, n_pairs * LANES), F32))
    return pl.pallas_call(
        functools.partial(_band_attn_kernel, dil=dil, nblk=nblk, max_dist=max_dist, sb_per_seq=seq // sb,
                          gqa=gqa, use_sinks=use_sinks, want_lse=want_lse),
        grid=(t // sb, n_pairs),
        in_specs=[
            smem,
            smem,
            pl.BlockSpec((sb, LANES), lambda i, p: (i, qb + p)),
            pl.BlockSpec((sb, LANES), kv_map),
            pl.BlockSpec((sb, LANES), vv_map),
            pl.BlockSpec((bw, LANES), kp_map),
            pl.BlockSpec((bw, LANES), vp_map),
        ],
        out_specs=out_specs,
        out_shape=out_shape,
        scratch_shapes=[pltpu.VMEM((bw + sb, LANES), BF16 if dil == 1 else F32)] * 2
        + [pltpu.VMEM((2, 2 * BLOCK, 2 * BLOCK), F32)],
        compiler_params=_cparams("parallel", "arbitrary"),
        name=f"band_attn_d{dil}" + ("_gqa" if gqa else ""),
    )(slopes, sinks, z, z, z, z, z)


def _gelu(x):
    return 0.5 * x * (1.0 + lax.erf(x * np.float32(np.sqrt(0.5))))


def _gmlp_kernel(u0_ref, u1_ref, v0_ref, v1_ref, g_ref, b_ref, ws_ref, bs_ref, o_ref):
    rows = u0_ref.shape[0]
    u = _gelu(jnp.concatenate([u0_ref[...], u1_ref[...]], axis=1))
    v = _gelu(jnp.concatenate([v0_ref[...], v1_ref[...]], axis=1))
    v = _layer_norm(v, g_ref[...], b_ref[...]).astype(BF16)
    t_idx = lax.broadcasted_iota(jnp.int32, (B_CHUNK, B_CHUNK), 0)
    s_idx = lax.broadcasted_iota(jnp.int32, (B_CHUNK, B_CHUNK), 1)
    causal = t_idx >= s_idx
    for grp in range(B_GROUPS):
        w = jnp.where(causal, ws_ref[grp], 0.0).astype(BF16)
        bias = bs_ref[:, grp:grp + 1]
        cols = slice(grp * B_CHUNK, (grp + 1) * B_CHUNK)
        for ch in range(rows // B_CHUNK):
            rs = slice(ch * B_CHUNK, (ch + 1) * B_CHUNK)
            mixed = _dot(w, v[rs, cols]) + bias
            o_ref[rs, cols] = (u[rs, cols] * mixed).astype(o_ref.dtype)


def _gmlp(z, ln_g, ln_b, w_s, b_s_t):
    t = z.shape[0]
    rows = min(512, t)
    half = B_WIDTH // 2
    ub, vb = COL_UB // half, COL_VB // half
    col = lambda j: pl.BlockSpec((rows, half), lambda i: (i, j))
    vec = pl.BlockSpec((1, B_WIDTH), lambda i: (0, 0))
    return pl.pallas_call(
        _gmlp_kernel,
        grid=(t // rows,),
        in_specs=[col(ub), col(ub + 1), col(vb), col(vb + 1), vec, vec,
                  pl.BlockSpec((B_GROUPS, B_CHUNK, B_CHUNK), lambda i: (0, 0, 0)),
                  pl.BlockSpec((B_CHUNK, B_GROUPS), lambda i: (0, 0))],
        out_specs=pl.BlockSpec((rows, B_WIDTH), lambda i: (i, 0)),
        out_shape=jax.ShapeDtypeStruct((t, B_WIDTH), BF16),
        compiler_params=_cparams("parallel"),
        name="spatial_gating",
    )(z, z, z, z, ln_g, ln_b, w_s, b_s_t)


def _hgrn_tables():
    n = HGRN_CHUNK
    t = np.arange(n)[:, None]
    i = np.arange(n)[None, :]
    tabs = []
    for h in HGRN_LEVELS:
        start = (t // h) * h
        upper = ((t // h) % 2) == 1
        q_tab = (i >= start) & (i <= t)
        k_tab = (i > t) & (i < start + h)
        tabs.append(np.where(upper, q_tab, k_tab))
    tabs.append(i <= t)
    tabs.append(i > t)
    w = np.concatenate(tabs, axis=0).astype(np.float32)
    s = np.arange(n)[None, :]
    x = np.bitwise_xor(t, s)
    level = np.where(s < t, np.floor(np.log2(np.maximum(x, 1))).astype(np.int32), np.where(s == t, -1, -2))
    return w, level.astype(np.int32)


def _hgrn_kernel(lbl_ref, ng_ref, w_ref, lvl_ref, q0_ref, q1_ref, f0_ref, f1_ref, i0_ref, i1_ref, g0_ref, g1_ref,
                 o_ref, st_ref, *, layer):
    n = HGRN_CHUNK

    @pl.when(pl.program_id(1) == 0)
    def _():
        st_ref[...] = jnp.zeros_like(st_ref)

    logits = lbl_ref[...]
    e = jnp.exp(logits - jnp.max(logits, axis=0, keepdims=True))
    probs = e / jnp.sum(e, axis=0, keepdims=True)
    lb = jnp.zeros((1, C_WIDTH), F32)
    for j in range(1, layer + 1):
        lb = lb + probs[j:j + 1, :]

    w = w_ref[...]
    lvl = lvl_ref[...]
    row = lax.broadcasted_iota(jnp.int32, (n, C_WIDTH), 0)
    ng = ng_ref[...]
    nl = len(HGRN_LEVELS)
    for ch in range(q0_ref.shape[0] // n):
        rs = slice(ch * n, (ch + 1) * n)
        both = lambda a_ref, b_ref: jnp.concatenate([a_ref[rs, :], b_ref[rs, :]], axis=1)
        z = both(f0_ref, f1_ref)
        q = both(q0_ref, q1_ref)
        f = lb + (1.0 - lb) * jax.nn.sigmoid(z)
        logf = jnp.log(jnp.maximum(f, F_MIN))
        kk = (1.0 - lb) * jax.nn.sigmoid(-z)
        vb = both(i0_ref, i1_ref).astype(BF16)
        l1 = logf.astype(BF16)
        l2 = (logf - l1.astype(F32)).astype(BF16)
        expo = _dot(w, l1) + _dot(w, l2)
        xs = []
        for li, h in enumerate(HGRN_LEVELS):
            upper = (row & h) != 0
            xs.append((jnp.where(upper, q, kk) * jnp.exp(expo[li * n:(li + 1) * n])).astype(BF16))
        x1 = jnp.where((row & 1) != 0, q * jnp.maximum(f, F_MIN), kk).astype(BF16)
        q_in = (q * jnp.exp(expo[nl * n:(nl + 1) * n])).astype(BF16)
        k_out = (kk * jnp.exp(expo[(nl + 1) * n:(nl + 2) * n])).astype(BF16)
        carry = jnp.exp(expo[(nl + 1) * n - 1:(nl + 1) * n])
        qb, kb = q.astype(BF16), kk.astype(BF16)
        outs = []
        for hd in range(C_HEADS):
            cs = slice(hd * C_DIM, (hd + 1) * C_DIM)
            scores = jnp.where(lvl == -1, _dot_nt(qb[:, cs], kb[:, cs]), 0.0)
            scores = scores + jnp.where(lvl == 0, _dot_nt(x1[:, cs], x1[:, cs]), 0.0)
            for li, h in enumerate(HGRN_LEVELS):
                scores = scores + jnp.where(lvl == int(np.log2(h)), _dot_nt(xs[li][:, cs], xs[li][:, cs]), 0.0)
            st = st_ref[hd]
            o = _dot(scores.astype(BF16), vb[:, cs]) + _dot_nt(q_in[:, cs], st.astype(BF16))
            st_ref[hd] = st * carry[:, cs] + _dot_tn(vb[:, cs], k_out[:, cs])
            outs.append(o * lax.rsqrt(jnp.mean(o * o, axis=-1, keepdims=True) + LN_EPS))
        o = jnp.concatenate(outs, axis=1) * ng
        o_ref[rs, :] = (o * jax.nn.sigmoid(both(g0_ref, g1_ref))).astype(o_ref.dtype)


def _hgrn(z, lb_logits, norm_g, layer, batch, seq):
    t = z.shape[0]
    rows = min(512, seq)
    per_seq = seq // rows
    half = C_WIDTH // 2
    w, lvl = _hgrn_tables()
    col = lambda c0, j: pl.BlockSpec((rows, half), lambda b, c: (b * per_seq + c, c0 // half + j))
    const = lambda shape: pl.BlockSpec(shape, lambda b, c: (0,) * len(shape))
    return pl.pallas_call(
        functools.partial(_hgrn_kernel, layer=layer),
        grid=(batch, per_seq),
        in_specs=[const(lb_logits.shape), const(norm_g.shape), const(w.shape), const(lvl.shape)]
        + [col(c0, j) for c0 in (COL_QC, COL_FC, COL_IC, COL_GC) for j in (0, 1)],
        out_specs=pl.BlockSpec((rows, C_WIDTH), lambda b, c: (b * per_seq + c, 0)),
        out_shape=jax.ShapeDtypeStruct((t, C_WIDTH), BF16),
        scratch_shapes=[pltpu.VMEM((C_HEADS, C_DIM, C_DIM), F32)],
        compiler_params=_cparams("parallel", "arbitrary"),
        name="hgrn2",
    )(lb_logits, norm_g, jnp.asarray(w, BF16), jnp.asarray(lvl), *([z] * 8))


def _merge_kernel(x_ref, oa_ref, ob_ref, oc_ref, d0_ref, d1_ref, d2_ref, l0_ref, l1_ref, l2_ref, gt_ref,
                  wa_ref, wb_ref, wc_ref, wd_ref, wo_ref, g_ref, b_ref, o_ref, obf_ref, y_ref, *, alpha):
    d = x_ref.shape[1]
    i = pl.program_id(0)
    nt = pl.num_programs(0) - 1

    def finish_previous():
        o = _layer_norm(y_ref[...], g_ref[...], b_ref[...])
        o_ref[...] = o
        obf_ref[...] = o.astype(BF16)

    @pl.when(i == 0)
    def _():
        y_ref[...] = jnp.zeros_like(y_ref)

    @pl.when(i < nt)
    def _():
        finish_previous()
        l0, l1, l2 = l0_ref[...], l1_ref[...], l2_ref[...]
        m = jnp.maximum(jnp.maximum(l0, l1), l2)
        e0, e1, e2 = jnp.exp(l0 - m), jnp.exp(l1 - m), jnp.exp(l2 - m)
        inv = 1.0 / (e0 + e1 + e2)
        od = (e0 * inv) * d0_ref[...] + (e1 * inv) * d1_ref[...] + (e2 * inv) * d2_ref[...]
        merged = gt_ref[:, 0:d].astype(F32) * _dot(oa_ref[...], wa_ref[...])
        merged = merged + gt_ref[:, d:2 * d].astype(F32) * _dot(ob_ref[...], wb_ref[...])
        merged = merged + gt_ref[:, 2 * d:3 * d].astype(F32) * _dot(oc_ref[...], wc_ref[...])
        merged = merged + gt_ref[:, 3 * d:4 * d].astype(F32) * _dot(od.astype(BF16), wd_ref[...])
        y_ref[...] = alpha * x_ref[...] + _dot(merged.astype(BF16), wo_ref[...])

    @pl.when(i == nt)
    def _():
        finish_previous()


def _merge(x, oa, ob, oc, ods, lses, gates, wa, wb, wc, wd, wo, layer, ln_g, ln_b, alpha):
    t, d = x.shape
    tm = min(256, t)
    nt = t // tm
    row = lambda width: pl.BlockSpec((tm, width), lambda i: (jnp.minimum(i, nt - 1), 0))
    lag = pl.BlockSpec((tm, d), lambda i: (jnp.maximum(i - 1, 0), 0))
    wspec = lambda k: pl.BlockSpec((None, k, d), lambda i: (layer, 0, 0), pipeline_mode=pl.Buffered(1))
    vec = pl.BlockSpec((1, d), lambda i: (0, 0))
    return pl.pallas_call(
        functools.partial(_merge_kernel, alpha=alpha),
        grid=(nt + 1,),
        in_specs=[row(d), row(A_WIDTH), row(B_WIDTH), row(C_WIDTH)] + [row(D_GROUP_WIDTH)] * 6
        + [row(N_BRANCHES * d), wspec(A_WIDTH), wspec(B_WIDTH), wspec(C_WIDTH), wspec(D_GROUP_WIDTH), wspec(d),
           vec, vec],
        out_specs=[lag, lag],
        out_shape=[jax.ShapeDtypeStruct((t, d), F32), jax.ShapeDtypeStruct((t, d), BF16)],
        scratch_shapes=[pltpu.VMEM((tm, d), F32)],
        compiler_params=_cparams("arbitrary"),
        name="merge_out_ln",
    )(x, oa, ob, oc, *ods, *lses, gates, wa, wb, wc, wd, wo, ln_g, ln_b)


def _ple_kernel(xb_ref, p_ref, wg_ref, wp_ref, o_ref):
    gate = jax.nn.sigmoid(_dot(xb_ref[...], wg_ref[...]))
    o_ref[...] = gate * _dot(p_ref[...].astype(BF16), wp_ref[...])


def _ple(xb, p, wg, wp, layer):
    t, d = xb.shape
    pd = p.shape[-1]
    tm = min(1024, t)
    tn = min(1024, d)
    return pl.pallas_call(
        _ple_kernel,
        grid=(d // tn, t // tm),
        in_specs=[
            pl.BlockSpec((tm, d), lambda n, m: (m, 0)),
            pl.BlockSpec((None, tm, pd), lambda n, m: (layer, m, 0)),
            pl.BlockSpec((None, d, tn), lambda n, m: (layer, 0, n)),
            pl.BlockSpec((None, pd, tn), lambda n, m: (layer, 0, n)),
        ],
        out_specs=pl.BlockSpec((tm, tn), lambda n, m: (m, n)),
        out_shape=jax.ShapeDtypeStruct((t, d), F32),
        compiler_params=_cparams("parallel", "arbitrary"),
        name="ple_gate",
    )(xb, p, wg, wp)


def _alibi_slopes(n):
    return jnp.exp2(-8.0 * jnp.arange(1, n + 1, dtype=F32) / n)


def kernel(x, p, ln_g, ln_b, ffn_w_gate, ffn_w_up, ffn_w_down, w_in, attn_sinks, gmlp_ln_g, gmlp_ln_b, gmlp_w_s,
           gmlp_b_s, hgrn_lb_logits, hgrn_norm_g, w_br_a, w_br_b, w_br_c, w_br_d, w_out, ple_w_proj, ple_w_gate):
    batch, seq, d = x.shape
    depth = ln_g.shape[0]
    t = batch * seq
    alpha = (2.0 * depth) ** 0.25
    assert seq % (BLOCK * D_PATTERNS[-1][1]) == 0, "the widest dilation needs whole 128-row residue blocks"
    assert w_in.shape[-1] == MIX_WIDTH + N_BRANCHES * d
    bf =lambda a: a.astype(BF16)
    wg, wu, wd = bf(ffn_w_gate), bf(ffn_w_up), bf(ffn_w_down)
    w_in_b = bf(w_in)
    wa, wb, wc, wdd, wo = bf(w_br_a), bf(w_br_b), bf(w_br_c), bf(w_br_d), bf(w_out)
    wpp, wpg = bf(ple_w_proj), bf(ple_w_gate)
    p2 = p.reshape(depth, t, p.shape[-1])
    slopes = _alibi_slopes(N_SOFTMAX_HEADS)
    no_sinks = jnp.zeros((D_HEADS,), F32)

    xf = x.reshape(t, d)
    for i in range(depth):
        xf, xb = _ffn(xf, wg, wu, wd, i, 0, ln_g[i, 0][None], ln_b[i, 0][None], None, alpha)
        z = _proj(xb, w_in_b, i, 0, MIX_WIDTH, F32, False)
        gates = _proj(xb, w_in_b, i, MIX_WIDTH, N_BRANCHES * d, BF16, True)
        oa = _band_attn(z, slopes[:A_Q_HEADS], attn_sinks[i], seq=seq, q_col=COL_QA, k_col=COL_KA, v_col=COL_VA,
                        n_pairs=A_Q_HEADS // 2, dil=1, max_dist=A_WINDOW - 1, gqa=True, use_sinks=True,
                        want_lse=False, out_dtype=BF16)[0]
        ob = _gmlp(z, gmlp_ln_g[i][None], gmlp_ln_b[i][None], gmlp_w_s[i], gmlp_b_s[i].T)
        oc = _hgrn(z, hgrn_lb_logits, hgrn_norm_g[i][None], i, batch, seq)
        ods, lses = [], []
        for g, (window, dil) in enumerate(D_PATTERNS):
            h0 = A_Q_HEADS + g * D_HEADS
            od, lse = _band_attn(z, slopes[h0:h0 + D_HEADS], no_sinks, seq=seq, q_col=COL_QD + g * D_GROUP_WIDTH,
                                 k_col=COL_KD + g * D_GROUP_WIDTH, v_col=COL_VD + g * D_GROUP_WIDTH,
                                 n_pairs=D_HEADS // 2, dil=dil, max_dist=window // dil, gqa=False, use_sinks=False,
                                 want_lse=True, out_dtype=F32)
            ods.append(od)
            lses.append(lse)
        xf, xb = _merge(xf, oa, ob, oc, ods, lses, gates, wa, wb, wc, wdd, wo, i, ln_g[i, 1][None], ln_b[i, 1][None],
                        alpha)
        ple = _ple(xb, p2, wpg, wpp, i)
        xf, xb = _ffn(xf, wg, wu, wd, i, 1, ln_g[i, 2][None], ln_b[i, 2][None], ple, alpha)
    return xf.reshape(batch, seq, d)
```

```python
import functools

import numpy as np

import jax
import jax.numpy as jnp
from jax import lax
from jax.experimental import pallas as pl
from jax.experimental.pallas import tpu as pltpu

F32 = jnp.float32
BF16 = jnp.bfloat16

LANES = 128
VMEM_LIMIT_BYTES = 56 * 1024 * 1024

HEAD_DIM = 64
BLOCK = 128
LN_EPS = 1e-5
NEG_BIG = -1e30
F_MIN = 1e-6
A_Q_HEADS = 8
A_KV_HEADS = 2
A_WINDOW = 128
B_GROUPS = 4
B_CHUNK = 128
C_HEADS = 4
C_DIM = 128
D_PATTERNS = ((128, 1), (512, 4), (2048, 16))
D_HEADS = 4
N_SOFTMAX_HEADS = A_Q_HEADS + len(D_PATTERNS) * D_HEADS
N_BRANCHES = 4
A_WIDTH = A_Q_HEADS * HEAD_DIM
A_KV_WIDTH = A_KV_HEADS * HEAD_DIM
B_WIDTH = B_GROUPS * B_CHUNK
C_WIDTH = C_HEADS * C_DIM
D_GROUP_WIDTH = D_HEADS * HEAD_DIM
D_WIDTH = len(D_PATTERNS) * D_GROUP_WIDTH
COL_QA = 0
COL_KA = COL_QA + A_WIDTH
COL_VA = COL_KA + A_KV_WIDTH
COL_UB = COL_VA + A_KV_WIDTH
COL_VB = COL_UB + B_WIDTH
COL_QC = COL_VB + B_WIDTH
COL_FC = COL_QC + C_WIDTH
COL_IC = COL_FC + C_WIDTH
COL_GC = COL_IC + C_WIDTH
COL_QD = COL_GC + C_WIDTH
COL_KD = COL_QD + D_WIDTH
COL_VD = COL_KD + D_WIDTH
MIX_WIDTH = COL_VD + D_WIDTH

ATTN_UNROLL = 8
HGRN_CHUNK = 128
HGRN_LEVELS = (64, 32, 16, 8, 4, 2)


def _cparams(*sem):
    return pltpu.CompilerParams(dimension_semantics=sem, vmem_limit_bytes=VMEM_LIMIT_BYTES)


def _layer_norm(v, g, b):
    mu = jnp.mean(v, axis=-1, keepdims=True)
    d = v - mu
    var = jnp.mean(d * d, axis=-1, keepdims=True)
    return d * lax.rsqrt(var + LN_EPS) * g + b


def _dot(a, b):
    return jnp.dot(a, b, preferred_element_type=F32)


def _dot_nt(a, b):
    return lax.dot_general(a, b, (((1,), (1,)), ((), ())), preferred_element_type=F32)


def _dot_tn(a, b):
    return lax.dot_general(a, b, (((0,), (0,)), ((), ())), preferred_element_type=F32)


def _ffn_kernel(x_ref, wg_ref, wu_ref, wd_ref, g_ref, b_ref, *rest, alpha, has_extra, nf):
    if has_extra:
        e_ref, o_ref, ob_ref, xb_ref, acc_ref = rest
    else:
        o_ref, ob_ref, xb_ref, acc_ref = rest
    i = pl.program_id(0)
    f = pl.program_id(1)
    live = i < pl.num_programs(0) - 1

    def contribution():
        xb = xb_ref[...]
        g = _dot(xb, wg_ref[...])
        u = _dot(xb, wu_ref[...])
        h = (g * jax.nn.sigmoid(g) * u).astype(BF16)
        return _dot(h, wd_ref[...])

    @pl.when((i == 0) & (f == 0))
    def _():
        acc_ref[...] = jnp.zeros_like(acc_ref)

    @pl.when(f == 0)
    def _():
        y = 0.5 * acc_ref[...]
        if has_extra:
            y = y + e_ref[...]
        o = _layer_norm(y, g_ref[...], b_ref[...])
        o_ref[...] = o
        ob_ref[...] = o.astype(BF16)
        xb_ref[...] = x_ref[...].astype(BF16)
        acc_ref[...] = (2.0 * alpha) * x_ref[...] + contribution()

    if nf > 1:
        @pl.when((f > 0) & live)
        def _():
            acc_ref[...] += contribution()


def _ffn(x, wg, wu, wd, layer, which, ln_g, ln_b, extra, alpha, tm=512, tf=512):
    t, d = x.shape
    ff = wg.shape[-1]
    tm = min(tm, t)
    tf = min(tf, ff)
    nt, nf = t // tm, ff // tf

    def rows(first, switch):
        return pl.BlockSpec((tm, d), lambda i, f: (jnp.clip(first(i) + jnp.where(f >= switch, 1, 0), 0, nt - 1), 0))

    cur = rows(lambda i: i, max(1, nf // 2))
    vec = pl.BlockSpec((1, d), lambda i, f: (0, 0))
    chunk = lambda i, f: jnp.where(i < nt, f, nf - 1)
    in_specs = [
        cur,
        pl.BlockSpec((None, None, d, tf), lambda i, f: (layer, which, 0, chunk(i, f))),
        pl.BlockSpec((None, None, d, tf), lambda i, f: (layer, which, 0, chunk(i, f))),
        pl.BlockSpec((None, None, tf, d), lambda i, f: (layer, which, chunk(i, f), 0)),
        vec,
        vec,
    ]
    args = [x, wg, wu, wd, ln_g, ln_b]
    if extra is not None:
        in_specs.append(rows(lambda i: i - 1, max(1, (3 * nf) // 4)))
        args.append(extra)
    return pl.pallas_call(
        functools.partial(_ffn_kernel, alpha=alpha, has_extra=extra is not None, nf=nf),
        grid=(nt + 1, nf),
        in_specs=in_specs,
        out_specs=[rows(lambda i: i - 1, 1), rows(lambda i: i - 1, 3)],
        out_shape=[jax.ShapeDtypeStruct((t, d), F32), jax.ShapeDtypeStruct((t, d), BF16)],
        scratch_shapes=[pltpu.VMEM((tm, d), BF16), pltpu.VMEM((tm, d), F32)],
        compiler_params=_cparams("arbitrary", "arbitrary"),
        name="ffn_ln",
    )(*args)


def _proj_kernel(x_ref, w_ref, o_ref, *, act):
    z = _dot(x_ref[...], w_ref[...])
    if act:
        z = jax.nn.sigmoid(z)
    o_ref[...] = z.astype(o_ref.dtype)


def _proj(xb, w, layer, col0, width, out_dtype, act):
    t, d = xb.shape
    tm = min(2048, t)
    tn = min(1024, width)
    off = col0 // tn
    return pl.pallas_call(
        functools.partial(_proj_kernel, act=act),
        grid=(width // tn, t // tm),
        in_specs=[
            pl.BlockSpec((tm, d), lambda n, m: (m, 0)),
            pl.BlockSpec((None, d, tn), lambda n, m: (layer, 0, n + off)),
        ],
        out_specs=pl.BlockSpec((tm, tn), lambda n, m: (m, n)),
        out_shape=jax.ShapeDtypeStruct((t, width), out_dtype),
        compiler_params=_cparams("parallel", "arbitrary"),
        name="in_proj_gates" if act else "in_proj_mix",
    )(xb, w)


def _band_attn_kernel(slope_ref, sink_ref, q_ref, k_ref, v_ref, kp_ref, vp_ref, *rest,
                      dil, nblk, max_dist, sb_per_seq, gqa, use_sinks, want_lse):
    if want_lse:
        o_ref, lse_ref, kbuf, vbuf, bias_ref = rest
    else:
        o_ref, kbuf, vbuf, bias_ref = rest
        lse_ref = None
    i = pl.program_id(0)
    p = pl.program_id(1)
    bw = BLOCK * dil
    first_in_seq = (i % sb_per_seq) == 0
    lane = lax.broadcasted_iota(jnp.int32, (1, LANES), 1)
    low = lane < HEAD_DIM

    def stage(cur_ref, prev_ref, buf):
        cur = cur_ref[...]
        prev = prev_ref[...]
        if gqa:
            kv_first = (p // 2) == 0

            def dup(a):
                r = pltpu.roll(a, HEAD_DIM, 1)
                return jnp.where(kv_first, jnp.where(low, a, r), jnp.where(low, r, a))

            cur, prev = dup(cur), dup(prev)
        buf[0:bw, :] = prev.astype(buf.dtype)
        buf[bw:, :] = cur.astype(buf.dtype)

    if gqa:
        @pl.when(p % 2 == 0)
        def _():
            stage(k_ref, kp_ref, kbuf)
            stage(v_ref, vp_ref, vbuf)
    else:
        stage(k_ref, kp_ref, kbuf)
        stage(v_ref, vp_ref, vbuf)

    q_off = lax.broadcasted_iota(jnp.int32, (2 * BLOCK, 2 * BLOCK), 0) % BLOCK
    k_off = lax.broadcasted_iota(jnp.int32, (2 * BLOCK, 2 * BLOCK), 1)
    dist = q_off + BLOCK - k_off
    in_band = (dist >= 0) & (dist <= max_dist)
    is_cur = k_off >= BLOCK
    dist_f = (jnp.maximum(dist, 0) * dil).astype(F32)
    top = lax.broadcasted_iota(jnp.int32, (2 * BLOCK, 1), 0) < BLOCK
    neg_slope = jnp.where(top, -slope_ref[2 * p], -slope_ref[2 * p + 1])
    bias = jnp.where(in_band, neg_slope * dist_f, NEG_BIG)
    bias_ref[0] = bias
    bias_ref[1] = jnp.where(is_cur, bias, NEG_BIG)
    if use_sinks:
        sink = jnp.where(top, sink_ref[2 * p], sink_ref[2 * p + 1])
    ones = jnp.ones((2 * BLOCK, LANES), BF16)

    def body(it, carry):
        j = it // dil
        c = it % dil
        row0 = j * bw + c
        if dil == 1:
            row0 = pl.multiple_of(row0, BLOCK)
            qs = pl.ds(row0, BLOCK)
            ks = pl.ds(row0, 2 * BLOCK)
        else:
            qs = pl.ds(row0, BLOCK, stride=dil)
            ks = pl.ds(row0, 2 * BLOCK, stride=dil)
        qc = q_ref[qs, :] * (HEAD_DIM ** -0.5)
        kk = kbuf[ks, :].astype(BF16)
        vv = vbuf[ks, :].astype(BF16)
        q2 = jnp.concatenate([jnp.where(low, qc, 0.0), jnp.where(low, 0.0, qc)], axis=0).astype(BF16)
        tbl = bias_ref[jnp.logical_and(j == 0, first_in_seq).astype(jnp.int32)]
        s = jnp.where(tbl > 0.5 * NEG_BIG, _dot_nt(q2, kk) + tbl, NEG_BIG)
        m = jnp.max(s, axis=-1, keepdims=True)
        if use_sinks:
            m = jnp.maximum(m, sink)
        prb = jnp.exp(s - m).astype(BF16)
        ov = _dot(prb, jnp.concatenate([vv, ones], axis=1))
        den = ov[:, LANES:]
        if use_sinks:
            den = den + jnp.exp(sink - m)
        on = ov[:, :LANES] / den
        o_ref[qs, :] = jnp.where(low, on[0:BLOCK], on[BLOCK:]).astype(o_ref.dtype)
        if want_lse:
            lse = m + jnp.log(den)
            lse_ref[qs, :] = jnp.where(low, lse[0:BLOCK], lse[BLOCK:])
        return carry

    lax.fori_loop(0, nblk * dil, body, 0, unroll=ATTN_UNROLL)


def _band_attn(z, slopes, sinks, *, seq, q_col, k_col, v_col, n_pairs, dil, max_dist, gqa, use_sinks,
               want_lse, out_dtype):
    t = z.shape[0]
    bw = BLOCK * dil
    sb = min(max(bw, 1024), seq)
    nblk = sb // bw
    qb, kb, vb = q_col // LANES, k_col // LANES, v_col // LANES
    if gqa:
        kv_map = lambda i, p: (i, kb)
        vv_map = lambda i, p: (i, vb)
        kp_map = lambda i, p: (jnp.maximum(i * nblk - 1, 0), kb)
        vp_map = lambda i, p: (jnp.maximum(i * nblk - 1, 0), vb)
    else:
        kv_map = lambda i, p: (i, kb + p)
        vv_map = lambda i, p: (i, vb + p)
        kp_map = lambda i, p: (jnp.maximum(i * nblk - 1, 0), kb + p)
        vp_map = lambda i, p: (jnp.maximum(i * nblk - 1, 0), vb + p)
    smem = pl.BlockSpec(memory_space=pltpu.SMEM)
    out_spec = pl.BlockSpec((sb, LANES), lambda i, p: (i, p))
    out_specs = [out_spec]
    out_shape = [jax.ShapeDtypeStruct((t, n_pairs * LANES), out_dtype)]
    if want_lse:
        out_specs.append(out_spec)
        out_shape.append(jax.ShapeDtypeStruct((t, n_pairs * LANES), F32))
    return pl.pallas_call(
        functools.partial(_band_attn_kernel, dil=dil, nblk=nblk, max_dist=max_dist, sb_per_seq=seq // sb,
                          gqa=gqa, use_sinks=use_sinks, want_lse=want_lse),
        grid=(t // sb, n_pairs),
        in_specs=[
            smem,
            smem,
            pl.BlockSpec((sb, LANES), lambda i, p: (i, qb + p)),
            pl.BlockSpec((sb, LANES), kv_map),
            pl.BlockSpec((sb, LANES), vv_map),
            pl.BlockSpec((bw, LANES), kp_map),
            pl.BlockSpec((bw, LANES), vp_map),
        ],
        out_specs=out_specs,
        out_shape=out_shape,
        scratch_shapes=[pltpu.VMEM((bw + sb, LANES), BF16 if dil == 1 else F32)] * 2
        + [pltpu.VMEM((2, 2 * BLOCK, 2 * BLOCK), F32)],
        compiler_params=_cparams("parallel", "arbitrary"),
        name=f"band_attn_d{dil}" + ("_gqa" if gqa else ""),
    )(slopes, sinks, z, z, z, z, z)


def _gelu(x):
    return 0.5 * x * (1.0 + lax.erf(x * np.float32(np.sqrt(0.5))))


def _gmlp_kernel(u0_ref, u1_ref, v0_ref, v1_ref, g_ref, b_ref, ws_ref, bs_ref, o_ref):
    rows = u0_ref.shape[0]
    u = _gelu(jnp.concatenate([u0_ref[...], u1_ref[...]], axis=1))
    v = _gelu(jnp.concatenate([v0_ref[...], v1_ref[...]], axis=1))
    v = _layer_norm(v, g_ref[...], b_ref[...]).astype(BF16)
    t_idx = lax.broadcasted_iota(jnp.int32, (B_CHUNK, B_CHUNK), 0)
    s_idx = lax.broadcasted_iota(jnp.int32, (B_CHUNK, B_CHUNK), 1)
    causal = t_idx >= s_idx
    for grp in range(B_GROUPS):
        w = jnp.where(causal, ws_ref[grp], 0.0).astype(BF16)
        bias = bs_ref[:, grp:grp + 1]
        cols = slice(grp * B_CHUNK, (grp + 1) * B_CHUNK)
        for ch in range(rows // B_CHUNK):
            rs = slice(ch * B_CHUNK, (ch + 1) * B_CHUNK)
            mixed = _dot(w, v[rs, cols]) + bias
            o_ref[rs, cols] = (u[rs, cols] * mixed).astype(o_ref.dtype)


def _gmlp(z, ln_g, ln_b, w_s, b_s_t):
    t = z.shape[0]
    rows = min(512, t)
    half = B_WIDTH // 2
    ub, vb = COL_UB // half, COL_VB // half
    col = lambda j: pl.BlockSpec((rows, half), lambda i: (i, j))
    vec = pl.BlockSpec((1, B_WIDTH), lambda i: (0, 0))
    return pl.pallas_call(
        _gmlp_kernel,
        grid=(t // rows,),
        in_specs=[col(ub), col(ub + 1), col(vb), col(vb + 1), vec, vec,
                  pl.BlockSpec((B_GROUPS, B_CHUNK, B_CHUNK), lambda i: (0, 0, 0)),
                  pl.BlockSpec((B_CHUNK, B_GROUPS), lambda i: (0, 0))],
        out_specs=pl.BlockSpec((rows, B_WIDTH), lambda i: (i, 0)),
        out_shape=jax.ShapeDtypeStruct((t, B_WIDTH), BF16),
        compiler_params=_cparams("parallel"),
        name="spatial_gating",
    )(z, z, z, z, ln_g, ln_b, w_s, b_s_t)


def _hgrn_tables():
    n = HGRN_CHUNK
    t = np.arange(n)[:, None]
    i = np.arange(n)[None, :]
    tabs = []
    for h in HGRN_LEVELS:
        start = (t // h) * h
        upper = ((t // h) % 2) == 1
        q_tab = (i >= start) & (i <= t)
        k_tab = (i > t) & (i < start + h)
        tabs.append(np.where(upper, q_tab, k_tab))
    tabs.append(i <= t)
    tabs.append(i > t)
    w = np.concatenate(tabs, axis=0).astype(np.float32)
    s = np.arange(n)[None, :]
    x = np.bitwise_xor(t, s)
    level = np.where(s < t, np.floor(np.log2(np.maximum(x, 1))).astype(np.int32), np.where(s == t, -1, -2))
    return w, level.astype(np.int32)


def _hgrn_kernel(lbl_ref, ng_ref, w_ref, lvl_ref, q0_ref, q1_ref, f0_ref, f1_ref, i0_ref, i1_ref, g0_ref, g1_ref,
                 o_ref, st_ref, *, layer):
    n = HGRN_CHUNK

    @pl.when(pl.program_id(1) == 0)
    def _():
        st_ref[...] = jnp.zeros_like(st_ref)

    logits = lbl_ref[...]
    e = jnp.exp(logits - jnp.max(logits, axis=0, keepdims=True))
    probs = e / jnp.sum(e, axis=0, keepdims=True)
    lb = jnp.zeros((1, C_WIDTH), F32)
    for j in range(1, layer + 1):
        lb = lb + probs[j:j + 1, :]

    w = w_ref[...]
    lvl = lvl_ref[...]
    row = lax.broadcasted_iota(jnp.int32, (n, C_WIDTH), 0)
    ng = ng_ref[...]
    nl = len(HGRN_LEVELS)
    for ch in range(q0_ref.shape[0] // n):
        rs = slice(ch * n, (ch + 1) * n)
        both = lambda a_ref, b_ref: jnp.concatenate([a_ref[rs, :], b_ref[rs, :]], axis=1)
        z = both(f0_ref, f1_ref)
        q = both(q0_ref, q1_ref)
        f = lb + (1.0 - lb) * jax.nn.sigmoid(z)
        logf = jnp.log(jnp.maximum(f, F_MIN))
        kk = (1.0 - lb) * jax.nn.sigmoid(-z)
        vb = both(i0_ref, i1_ref).astype(BF16)
        l1 = logf.astype(BF16)
        l2 = (logf - l1.astype(F32)).astype(BF16)
        expo = _dot(w, l1) + _dot(w, l2)
        xs = []
        for li, h in enumerate(HGRN_LEVELS):
            upper = (row & h) != 0
            xs.append((jnp.where(upper, q, kk) * jnp.exp(expo[li * n:(li + 1) * n])).astype(BF16))
        x1 = jnp.where((row & 1) != 0, q * jnp.maximum(f, F_MIN), kk).astype(BF16)
        q_in = (q * jnp.exp(expo[nl * n:(nl + 1) * n])).astype(BF16)
        k_out = (kk * jnp.exp(expo[(nl + 1) * n:(nl + 2) * n])).astype(BF16)
        carry = jnp.exp(expo[(nl + 1) * n - 1:(nl + 1) * n])
        qb, kb = q.astype(BF16), kk.astype(BF16)
        outs = []
        for hd in range(C_HEADS):
            cs = slice(hd * C_DIM, (hd + 1) * C_DIM)
            scores = jnp.where(lvl == -1, _dot_nt(qb[:, cs], kb[:, cs]), 0.0)
            scores = scores + jnp.where(lvl == 0, _dot_nt(x1[:, cs], x1[:, cs]), 0.0)
            for li, h in enumerate(HGRN_LEVELS):
                scores = scores + jnp.where(lvl == int(np.log2(h)), _dot_nt(xs[li][:, cs], xs[li][:, cs]), 0.0)
            st = st_ref[hd]
            o = _dot(scores.astype(BF16), vb[:, cs]) + _dot_nt(q_in[:, cs], st.astype(BF16))
            st_ref[hd] = st * carry[:, cs] + _dot_tn(vb[:, cs], k_out[:, cs])
            outs.append(o * lax.rsqrt(jnp.mean(o * o, axis=-1, keepdims=True) + LN_EPS))
        o = jnp.concatenate(outs, axis=1) * ng
        o_ref[rs, :] = (o * jax.nn.sigmoid(both(g0_ref, g1_ref))).astype(o_ref.dtype)


def _hgrn(z, lb_logits, norm_g, layer, batch, seq):
    t = z.shape[0]
    rows = min(512, seq)
    per_seq = seq // rows
    half = C_WIDTH // 2
    w, lvl = _hgrn_tables()
    col = lambda c0, j: pl.BlockSpec((rows, half), lambda b, c: (b * per_seq + c, c0 // half + j))
    const = lambda shape: pl.BlockSpec(shape, lambda b, c: (0,) * len(shape))
    return pl.pallas_call(
        functools.partial(_hgrn_kernel, layer=layer),
        grid=(batch, per_seq),
        in_specs=[const(lb_logits.shape), const(norm_g.shape), const(w.shape), const(lvl.shape)]
        + [col(c0, j) for c0 in (COL_QC, COL_FC, COL_IC, COL_GC) for j in (0, 1)],
        out_specs=pl.BlockSpec((rows, C_WIDTH), lambda b, c: (b * per_seq + c, 0)),
        out_shape=jax.ShapeDtypeStruct((t, C_WIDTH), BF16),
        scratch_shapes=[pltpu.VMEM((C_HEADS, C_DIM, C_DIM), F32)],
        compiler_params=_cparams("parallel", "arbitrary"),
        name="hgrn2",
    )(lb_logits, norm_g, jnp.asarray(w, BF16), jnp.asarray(lvl), *([z] * 8))


def _merge_kernel(x_ref, oa_ref, ob_ref, oc_ref, d0_ref, d1_ref, d2_ref, l0_ref, l1_ref, l2_ref, gt_ref,
                  wa_ref, wb_ref, wc_ref, wd_ref, wo_ref, g_ref, b_ref, o_ref, obf_ref, y_ref, *, alpha):
    d = x_ref.shape[1]
    i = pl.program_id(0)
    nt = pl.num_programs(0) - 1

    def finish_previous():
        o = _layer_norm(y_ref[...], g_ref[...], b_ref[...])
        o_ref[...] = o
        obf_ref[...] = o.astype(BF16)

    @pl.when(i == 0)
    def _():
        y_ref[...] = jnp.zeros_like(y_ref)

    @pl.when(i < nt)
    def _():
        finish_previous()
        l0, l1, l2 = l0_ref[...], l1_ref[...], l2_ref[...]
        m = jnp.maximum(jnp.maximum(l0, l1), l2)
        e0, e1, e2 = jnp.exp(l0 - m), jnp.exp(l1 - m), jnp.exp(l2 - m)
        inv = 1.0 / (e0 + e1 + e2)
        od = (e0 * inv) * d0_ref[...] + (e1 * inv) * d1_ref[...] + (e2 * inv) * d2_ref[...]
        merged = gt_ref[:, 0:d].astype(F32) * _dot(oa_ref[...], wa_ref[...])
        merged = merged + gt_ref[:, d:2 * d].astype(F32) * _dot(ob_ref[...], wb_ref[...])
        merged = merged + gt_ref[:, 2 * d:3 * d].astype(F32) * _dot(oc_ref[...], wc_ref[...])
        merged = merged + gt_ref[:, 3 * d:4 * d].astype(F32) * _dot(od.astype(BF16), wd_ref[...])
        y_ref[...] = alpha * x_ref[...] + _dot(merged.astype(BF16), wo_ref[...])

    @pl.when(i == nt)
    def _():
        finish_previous()


def _merge(x, oa, ob, oc, ods, lses, gates, wa, wb, wc, wd, wo, layer, ln_g, ln_b, alpha):
    t, d = x.shape
    tm = min(256, t)
    nt = t // tm
    row = lambda width: pl.BlockSpec((tm, width), lambda i: (jnp.minimum(i, nt - 1), 0))
    lag = pl.BlockSpec((tm, d), lambda i: (jnp.maximum(i - 1, 0), 0))
    wspec = lambda k: pl.BlockSpec((None, k, d), lambda i: (layer, 0, 0), pipeline_mode=pl.Buffered(1))
    vec = pl.BlockSpec((1, d), lambda i: (0, 0))
    return pl.pallas_call(
        functools.partial(_merge_kernel, alpha=alpha),
        grid=(nt + 1,),
        in_specs=[row(d), row(A_WIDTH), row(B_WIDTH), row(C_WIDTH)] + [row(D_GROUP_WIDTH)] * 6
        + [row(N_BRANCHES * d), wspec(A_WIDTH), wspec(B_WIDTH), wspec(C_WIDTH), wspec(D_GROUP_WIDTH), wspec(d),
           vec, vec],
        out_specs=[lag, lag],
        out_shape=[jax.ShapeDtypeStruct((t, d), F32), jax.ShapeDtypeStruct((t, d), BF16)],
        scratch_shapes=[pltpu.VMEM((tm, d), F32)],
        compiler_params=_cparams("arbitrary"),
        name="merge_out_ln",
    )(x, oa, ob, oc, *ods, *lses, gates, wa, wb, wc, wd, wo, ln_g, ln_b)


def _ple_kernel(xb_ref, p_ref, wg_ref, wp_ref, o_ref):
    gate = jax.nn.sigmoid(_dot(xb_ref[...], wg_ref[...]))
    o_ref[...] = gate * _dot(p_ref[...].astype(BF16), wp_ref[...])


def _ple(xb, p, wg, wp, layer):
    t, d = xb.shape
    pd = p.shape[-1]
    tm = min(1024, t)
    tn = min(1024, d)
    return pl.pallas_call(
        _ple_kernel,
        grid=(d // tn, t // tm),
        in_specs=[
            pl.BlockSpec((tm, d), lambda n, m: (m, 0)),
            pl.BlockSpec((None, tm, pd), lambda n, m: (layer, m, 0)),
            pl.BlockSpec((None, d, tn), lambda n, m: (layer, 0, n)),
            pl.BlockSpec((None, pd, tn), lambda n, m: (layer, 0, n)),
        ],
        out_specs=pl.BlockSpec((tm, tn), lambda n, m: (m, n)),
        out_shape=jax.ShapeDtypeStruct((t, d), F32),
        compiler_params=_cparams("parallel", "arbitrary"),
        name="ple_gate",
    )(xb, p, wg, wp)


def _alibi_slopes(n):
    return jnp.exp2(-8.0 * jnp.arange(1, n + 1, dtype=F32) / n)


def kernel(x, p, ln_g, ln_b, ffn_w_gate, ffn_w_up, ffn_w_down, w_in, attn_sinks, gmlp_ln_g, gmlp_ln_b, gmlp_w_s,
           gmlp_b_s, hgrn_lb_logits, hgrn_norm_g, w_br_a, w_br_b, w_br_c, w_br_d, w_out, ple_w_proj, ple_w_gate):
    batch, seq, d = x.shape
    depth = ln_g.shape[0]
    t = batch * seq
    alpha = (2.0 * depth) ** 0.25
    assert seq % (BLOCK * D_PATTERNS[-1][1]) == 0, "the widest dilation needs whole 128-row residue blocks"
    assert w_in.shape[-1] == MIX_WIDTH + N_BRANCHES * d
    bf =lambda a: a.astype(BF16)
    wg, wu, wd = bf(ffn_w_gate), bf(ffn_w_up), bf(ffn_w_down)
    w_in_b = bf(w_in)
    wa, wb, wc, wdd, wo = bf(w_br_a), bf(w_br_b), bf(w_br_c), bf(w_br_d), bf(w_out)
    wpp, wpg = bf(ple_w_proj), bf(ple_w_gate)
    p2 = p.reshape(depth, t, p.shape[-1])
    slopes = _alibi_slopes(N_SOFTMAX_HEADS)
    no_sinks = jnp.zeros((D_HEADS,), F32)

    xf = x.reshape(t, d)
    for i in range(depth):
        xf, xb = _ffn(xf, wg, wu, wd, i, 0, ln_g[i, 0][None], ln_b[i, 0][None], None, alpha)
        z = _proj(xb, w_in_b, i, 0, MIX_WIDTH, F32, False)
        gates = _proj(xb, w_in_b, i, MIX_WIDTH, N_BRANCHES * d, BF16, True)
        oa = _band_attn(z, slopes[:A_Q_HEADS], attn_sinks[i], seq=seq, q_col=COL_QA, k_col=COL_KA, v_col=COL_VA,
                        n_pairs=A_Q_HEADS // 2, dil=1, max_dist=A_WINDOW - 1, gqa=True, use_sinks=True,
                        want_lse=False, out_dtype=BF16)[0]
        ob = _gmlp(z, gmlp_ln_g[i][None], gmlp_ln_b[i][None], gmlp_w_s[i], gmlp_b_s[i].T)
        oc = _hgrn(z, hgrn_lb_logits, hgrn_norm_g[i][None], i, batch, seq)
        ods, lses = [], []
        for g, (window, dil) in enumerate(D_PATTERNS):
            h0 = A_Q_HEADS + g * D_HEADS
            od, lse = _band_attn(z, slopes[h0:h0 + D_HEADS], no_sinks, seq=seq, q_col=COL_QD + g * D_GROUP_WIDTH,
                                 k_col=COL_KD + g * D_GROUP_WIDTH, v_col=COL_VD + g * D_GROUP_WIDTH,
                                 n_pairs=D_HEADS // 2, dil=dil, max_dist=window // dil, gqa=False, use_sinks=False,
                                 want_lse=True, out_dtype=F32)
            ods.append(od)
            lses.append(lse)
        xf, xb = _merge(xf, oa, ob, oc, ods, lses, gates, wa, wb, wc, wdd, wo, i, ln_g[i, 1][None], ln_b[i, 1][None],
                        alpha)
        ple = _ple(xb, p2, wpg, wpp, i)
        xf, xb = _ffn(xf, wg, wu, wd, i, 1, ln_g[i, 2][None], ln_b[i, 2][None], ple, alpha)
    return xf.reshape(batch, seq, d)
```

```python
import functools

import numpy as np

import jax
import jax.numpy as jnp
from jax import lax
from jax.experimental import pallas as pl
from jax.experimental.pallas import tpu as pltpu

F32 = jnp.float32
BF16 = jnp.bfloat16

LANES = 128
VMEM_LIMIT_BYTES = 56 * 1024 * 1024

HEAD_DIM = 64
BLOCK = 128
LN_EPS = 1e-5
NEG_BIG = -1e30
F_MIN = 1e-6
A_Q_HEADS = 8
A_KV_HEADS = 2
A_WINDOW = 128
B_GROUPS = 4
B_CHUNK = 128
C_HEADS = 4
C_DIM = 128
D_PATTERNS = ((128, 1), (512, 4), (2048, 16))
D_HEADS = 4
N_SOFTMAX_HEADS = A_Q_HEADS + len(D_PATTERNS) * D_HEADS
N_BRANCHES = 4
A_WIDTH = A_Q_HEADS * HEAD_DIM
A_KV_WIDTH = A_KV_HEADS * HEAD_DIM
B_WIDTH = B_GROUPS * B_CHUNK
C_WIDTH = C_HEADS * C_DIM
D_GROUP_WIDTH = D_HEADS * HEAD_DIM
D_WIDTH = len(D_PATTERNS) * D_GROUP_WIDTH
COL_QA = 0
COL_KA = COL_QA + A_WIDTH
COL_VA = COL_KA + A_KV_WIDTH
COL_UB = COL_VA + A_KV_WIDTH
COL_VB = COL_UB + B_WIDTH
COL_QC = COL_VB + B_WIDTH
COL_FC = COL_QC + C_WIDTH
COL_IC = COL_FC + C_WIDTH
COL_GC = COL_IC + C_WIDTH
COL_QD = COL_GC + C_WIDTH
COL_KD = COL_QD + D_WIDTH
COL_VD = COL_KD + D_WIDTH
MIX_WIDTH = COL_VD + D_WIDTH

ATTN_UNROLL = 8
HGRN_CHUNK = 128
HGRN_LEVELS = (64, 32, 16, 8, 4, 2)


def _cparams(*sem):
    return pltpu.CompilerParams(dimension_semantics=sem, vmem_limit_bytes=VMEM_LIMIT_BYTES)


def _layer_norm(v, g, b):
    mu = jnp.mean(v, axis=-1, keepdims=True)
    d = v - mu
    var = jnp.mean(d * d, axis=-1, keepdims=True)
    return d * lax.rsqrt(var + LN_EPS) * g + b


def _dot(a, b):
    return jnp.dot(a, b, preferred_element_type=F32)


def _dot_nt(a, b):
    return lax.dot_general(a, b, (((1,), (1,)), ((), ())), preferred_element_type=F32)


def _dot_tn(a, b):
    return lax.dot_general(a, b, (((0,), (0,)), ((), ())), preferred_element_type=F32)


def _ffn_kernel(x_ref, wg_hbm, wu_hbm, wd_hbm, g_ref, b_ref, *rest, alpha, has_extra, layer, which, nt, nf, tf):
    if has_extra:
        e_ref, o_ref, ob_ref, xb_ref, acc_ref, wgu_buf, wd_buf, sem = rest
    else:
        o_ref, ob_ref, xb_ref, acc_ref, wgu_buf, wd_buf, sem = rest
    i = pl.program_id(0)
    base = i * nf

    def copies(pos):
        slot = pos % 2
        off = pl.multiple_of((pos % nf) * tf, tf)
        gate_dst = wgu_buf.at[slot, :, pl.ds(0, tf)]
        up_dst = wgu_buf.at[slot, :, pl.ds(tf, tf)]
        return (
            pltpu.make_async_copy(wg_hbm.at[layer, which, :, pl.ds(off, tf)], gate_dst, sem.at[0, slot]),
            pltpu.make_async_copy(wu_hbm.at[layer, which, :, pl.ds(off, tf)], up_dst, sem.at[1, slot]),
            pltpu.make_async_copy(wd_hbm.at[layer, which, pl.ds(off, tf), :], wd_buf.at[slot], sem.at[2, slot]),
        )

    def contribution(pos):
        for c in copies(pos):
            c.wait()

        @pl.when(pos + 1 < nt * nf)
        def _():
            for c in copies(pos + 1):
                c.start()

        slot = pos % 2
        gu = _dot(xb_ref[...], wgu_buf[slot])
        g, u = gu[:, :tf], gu[:, tf:]
        h = (g * jax.nn.sigmoid(g) * u).astype(BF16)
        return _dot(h, wd_buf[slot])

    def finish_previous():
        y = 0.5 * acc_ref[...]
        if has_extra:
            y = y + e_ref[...]
        o = _layer_norm(y, g_ref[...], b_ref[...])
        o_ref[...] = o
        ob_ref[...] = o.astype(BF16)

    @pl.when(i == 0)
    def _():
        acc_ref[...] = jnp.zeros_like(acc_ref)
        for c in copies(0):
            c.start()

    @pl.when(i < nt)
    def _():
        finish_previous()
        xb_ref[...] = x_ref[...].astype(BF16)
        c = contribution(base)
        acc_ref[...] = (2.0 * alpha) * x_ref[...] + c

        def chunk(f, carry):
            c = contribution(base + f)
            acc_ref[...] = acc_ref[...] + c
            return carry

        lax.fori_loop(1, nf, chunk, 0)

    @pl.when(i == nt)
    def _():
        finish_previous()


def _ffn(x, wg, wu, wd, layer, which, ln_g, ln_b, extra, alpha, tm=512, tf=512):
    t, d = x.shape
    ff = wg.shape[-1]
    tm = min(tm, t)
    tf = min(tf, ff)
    nt, nf = t // tm, ff // tf
    cur = pl.BlockSpec((tm, d), lambda i: (jnp.minimum(i, nt - 1), 0))
    lag = pl.BlockSpec((tm, d), lambda i: (jnp.maximum(i - 1, 0), 0))
    vec = pl.BlockSpec((1, d), lambda i: (0, 0))
    hbm = pl.BlockSpec(memory_space=pl.ANY)
    in_specs = [cur, hbm, hbm, hbm, vec, vec]
    args = [x, wg, wu, wd, ln_g, ln_b]
    if extra is not None:
        in_specs.append(lag)
        args.append(extra)
    return pl.pallas_call(
        functools.partial(_ffn_kernel, alpha=alpha, has_extra=extra is not None, layer=layer, which=which, nt=nt,
                          nf=nf, tf=tf),
        grid=(nt + 1,),
        in_specs=in_specs,
        out_specs=[lag, lag],
        out_shape=[jax.ShapeDtypeStruct((t, d), F32), jax.ShapeDtypeStruct((t, d), BF16)],
        scratch_shapes=[pltpu.VMEM((tm, d), BF16), pltpu.VMEM((tm, d), F32), pltpu.VMEM((2, d, 2 * tf), BF16),
                        pltpu.VMEM((2, tf, d), BF16), pltpu.SemaphoreType.DMA((3, 2))],
        compiler_params=_cparams("arbitrary"),
        name="ffn_ln",
    )(*args)


def _proj_kernel(x_ref, w_ref, o_ref, *, act):
    z = _dot(x_ref[...], w_ref[...])
    if act:
        z = jax.nn.sigmoid(z)
    o_ref[...] = z.astype(o_ref.dtype)


def _proj(xb, w, layer, col0, width, out_dtype, act):
    t, d = xb.shape
    tm = min(2048, t)
    tn = min(1024, width)
    off = col0 // tn
    return pl.pallas_call(
        functools.partial(_proj_kernel, act=act),
        grid=(width // tn, t // tm),
        in_specs=[
            pl.BlockSpec((tm, d), lambda n, m: (m, 0)),
            pl.BlockSpec((None, d, tn), lambda n, m: (layer, 0, n + off)),
        ],
        out_specs=pl.BlockSpec((tm, tn), lambda n, m: (m, n)),
        out_shape=jax.ShapeDtypeStruct((t, width), out_dtype),
        compiler_params=_cparams("parallel", "arbitrary"),
        name="in_proj_gates" if act else "in_proj_mix",
    )(xb, w)


def _band_attn_kernel(slope_ref, sink_ref, q_ref, k_ref, v_ref, kp_ref, vp_ref, *rest,
                      dil, nblk, max_dist, sb_per_seq, gqa, use_sinks, want_lse):
    if want_lse:
        o_ref, lse_ref, kbuf, vbuf, bias_ref = rest
    else:
        o_ref, kbuf, vbuf, bias_ref = rest
        lse_ref = None
    i = pl.program_id(0)
    p = pl.program_id(1)
    bw = BLOCK * dil
    first_in_seq = (i % sb_per_seq) == 0
    lane = lax.broadcasted_iota(jnp.int32, (1, LANES), 1)
    low = lane < HEAD_DIM

    def stage(cur_ref, prev_ref, buf):
        cur = cur_ref[...]
        prev = prev_ref[...]
        if gqa:
            kv_first = (p // 2) == 0

            def dup(a):
                r = pltpu.roll(a, HEAD_DIM, 1)
                return jnp.where(kv_first, jnp.where(low, a, r), jnp.where(low, r, a))

            cur, prev = dup(cur), dup(prev)
        buf[0:bw, :] = prev.astype(buf.dtype)
        buf[bw:, :] = cur.astype(buf.dtype)

    if gqa:
        @pl.when(p % 2 == 0)
        def _():
            stage(k_ref, kp_ref, kbuf)
            stage(v_ref, vp_ref, vbuf)
    else:
        stage(k_ref, kp_ref, kbuf)
        stage(v_ref, vp_ref, vbuf)

    q_off = lax.broadcasted_iota(jnp.int32, (2 * BLOCK, 2 * BLOCK), 0) % BLOCK
    k_off = lax.broadcasted_iota(jnp.int32, (2 * BLOCK, 2 * BLOCK), 1)
    dist = q_off + BLOCK - k_off
    in_band = (dist >= 0) & (dist <= max_dist)
    is_cur = k_off >= BLOCK
    dist_f = (jnp.maximum(dist, 0) * dil).astype(F32)
    top = lax.broadcasted_iota(jnp.int32, (2 * BLOCK, 1), 0) < BLOCK
    neg_slope = jnp.where(top, -slope_ref[2 * p], -slope_ref[2 * p + 1])
    bias = jnp.where(in_band, neg_slope * dist_f, NEG_BIG)
    bias_ref[0] = bias
    bias_ref[1] = jnp.where(is_cur, bias, NEG_BIG)
    if use_sinks:
        sink = jnp.where(top, sink_ref[2 * p], sink_ref[2 * p + 1])
    ones = jnp.ones((2 * BLOCK, LANES), BF16)

    def body(it, carry):
        j = it // dil
        c = it % dil
        row0 = j * bw + c
        if dil == 1:
            row0 = pl.multiple_of(row0, BLOCK)
            qs = pl.ds(row0, BLOCK)
            ks = pl.ds(row0, 2 * BLOCK)
        else:
            qs = pl.ds(row0, BLOCK, stride=dil)
            ks = pl.ds(row0, 2 * BLOCK, stride=dil)
        qc = q_ref[qs, :] * (HEAD_DIM ** -0.5)
        kk = kbuf[ks, :].astype(BF16)
        vv = vbuf[ks, :].astype(BF16)
        q2 = jnp.concatenate([jnp.where(low, qc, 0.0), jnp.where(low, 0.0, qc)], axis=0).astype(BF16)
        tbl = bias_ref[jnp.logical_and(j == 0, first_in_seq).astype(jnp.int32)]
        s = jnp.where(tbl > 0.5 * NEG_BIG, _dot_nt(q2, kk) + tbl, NEG_BIG)
        m = jnp.max(s, axis=-1, keepdims=True)
        if use_sinks:
            m = jnp.maximum(m, sink)
        prb = jnp.exp(s - m).astype(BF16)
        ov = _dot(prb, jnp.concatenate([vv, ones], axis=1))
        den = ov[:, LANES:]
        if use_sinks:
            den = den + jnp.exp(sink - m)
        on = ov[:, :LANES] / den
        o_ref[qs, :] = jnp.where(low, on[0:BLOCK], on[BLOCK:]).astype(o_ref.dtype)
        if want_lse:
            lse = m + jnp.log(den)
            lse_ref[qs, :] = jnp.where(low, lse[0:BLOCK], lse[BLOCK:])
        return carry

    lax.fori_loop(0, nblk * dil, body, 0, unroll=ATTN_UNROLL)


def _band_attn(z, slopes, sinks, *, seq, q_col, k_col, v_col, n_pairs, dil, max_dist, gqa, use_sinks,
               want_lse, out_dtype):
    t = z.shape[0]
    bw = BLOCK * dil
    sb = min(max(2 * bw, 2048), seq)
    nblk = sb // bw
    qb, kb, vb = q_col // LANES, k_col // LANES, v_col // LANES
    if gqa:
        kv_map = lambda i, p: (i, kb)
        vv_map = lambda i, p: (i, vb)
        kp_map = lambda i, p: (jnp.maximum(i * nblk - 1, 0), kb)
        vp_map = lambda i, p: (jnp.maximum(i * nblk - 1, 0), vb)
    else:
        kv_map = lambda i, p: (i, kb + p)
        vv_map = lambda i, p: (i, vb + p)
        kp_map = lambda i, p: (jnp.maximum(i * nblk - 1, 0), kb + p)
        vp_map = lambda i, p: (jnp.maximum(i * nblk - 1, 0), vb + p)
    smem = pl.BlockSpec(memory_space=pltpu.SMEM)
    out_spec = pl.BlockSpec((sb, LANES), lambda i, p: (i, p))
    out_specs = [out_spec]
    out_shape = [jax.ShapeDtypeStruct((t, n_pairs * LANES), out_dtype)]
    if want_lse:
        out_specs.append(out_spec)
        out_shape.append(jax.ShapeDtypeStruct((t, n_pairs * LANES), F32))
    return pl.pallas_call(
        functools.partial(_band_attn_kernel, dil=dil, nblk=nblk, max_dist=max_dist, sb_per_seq=seq // sb,
                          gqa=gqa, use_sinks=use_sinks, want_lse=want_lse),
        grid=(t // sb, n_pairs),
        in_specs=[
            smem,
            smem,
            pl.BlockSpec((sb, LANES), lambda i, p: (i, qb + p)),
            pl.BlockSpec((sb, LANES), kv_map),
            pl.BlockSpec((sb, LANES), vv_map),
            pl.BlockSpec((bw, LANES), kp_map),
            pl.BlockSpec((bw, LANES), vp_map),
        ],
        out_specs=out_specs,
        out_shape=out_shape,
        scratch_shapes=[pltpu.VMEM((bw + sb, LANES), BF16 if dil == 1 else F32)] * 2
        + [pltpu.VMEM((2, 2 * BLOCK, 2 * BLOCK), F32)],
        compiler_params=_cparams("parallel", "arbitrary"),
        name=f"band_attn_d{dil}" + ("_gqa" if gqa else ""),
    )(slopes, sinks, z, z, z, z, z)


def _gelu(x):
    return 0.5 * x * (1.0 + lax.erf(x * np.float32(np.sqrt(0.5))))


def _gmlp_kernel(u0_ref, u1_ref, v0_ref, v1_ref, g_ref, b_ref, ws_ref, bs_ref, o_ref):
    rows = u0_ref.shape[0]
    u = _gelu(jnp.concatenate([u0_ref[...], u1_ref[...]], axis=1))
    v = _gelu(jnp.concatenate([v0_ref[...], v1_ref[...]], axis=1))
    v = _layer_norm(v, g_ref[...], b_ref[...]).astype(BF16)
    t_idx = lax.broadcasted_iota(jnp.int32, (B_CHUNK, B_CHUNK), 0)
    s_idx = lax.broadcasted_iota(jnp.int32, (B_CHUNK, B_CHUNK), 1)
    causal = t_idx >= s_idx
    for grp in range(B_GROUPS):
        w = jnp.where(causal, ws_ref[grp], 0.0).astype(BF16)
        bias = bs_ref[:, grp:grp + 1]
        cols = slice(grp * B_CHUNK, (grp + 1) * B_CHUNK)
        for ch in range(rows // B_CHUNK):
            rs = slice(ch * B_CHUNK, (ch + 1) * B_CHUNK)
            mixed = _dot(w, v[rs, cols]) + bias
            o_ref[rs, cols] = (u[rs, cols] * mixed).astype(o_ref.dtype)


def _gmlp(z, ln_g, ln_b, w_s, b_s_t):
    t = z.shape[0]
    rows = min(2048, t)
    half = B_WIDTH // 2
    ub, vb = COL_UB // half, COL_VB // half
    col = lambda j: pl.BlockSpec((rows, half), lambda i: (i, j))
    vec = pl.BlockSpec((1, B_WIDTH), lambda i: (0, 0))
    return pl.pallas_call(
        _gmlp_kernel,
        grid=(t // rows,),
        in_specs=[col(ub), col(ub + 1), col(vb), col(vb + 1), vec, vec,
                  pl.BlockSpec((B_GROUPS, B_CHUNK, B_CHUNK), lambda i: (0, 0, 0)),
                  pl.BlockSpec((B_CHUNK, B_GROUPS), lambda i: (0, 0))],
        out_specs=pl.BlockSpec((rows, B_WIDTH), lambda i: (i, 0)),
        out_shape=jax.ShapeDtypeStruct((t, B_WIDTH), BF16),
        compiler_params=_cparams("parallel"),
        name="spatial_gating",
    )(z, z, z, z, ln_g, ln_b, w_s, b_s_t)


def _hgrn_tables():
    n = HGRN_CHUNK
    t = np.arange(n)[:, None]
    i = np.arange(n)[None, :]
    tabs = []
    for h in HGRN_LEVELS:
        start = (t // h) * h
        upper = ((t // h) % 2) == 1
        q_tab = (i >= start) & (i <= t)
        k_tab = (i > t) & (i < start + h)
        tabs.append(np.where(upper, q_tab, k_tab))
    tabs.append(i <= t)
    tabs.append(i > t)
    w = np.concatenate(tabs, axis=0).astype(np.float32)
    s = np.arange(n)[None, :]
    x = np.bitwise_xor(t, s)
    level = np.where(s < t, np.floor(np.log2(np.maximum(x, 1))).astype(np.int32), np.where(s == t, -1, -2))
    return w, level.astype(np.int32)


def _hgrn_kernel(lbl_ref, ng_ref, w_ref, lvl_ref, q0_ref, q1_ref, f0_ref, f1_ref, i0_ref, i1_ref, g0_ref, g1_ref,
                 o_ref, st_ref, *, layer):
    n = HGRN_CHUNK

    @pl.when(pl.program_id(1) == 0)
    def _():
        st_ref[...] = jnp.zeros_like(st_ref)

    logits = lbl_ref[...]
    e = jnp.exp(logits - jnp.max(logits, axis=0, keepdims=True))
    probs = e / jnp.sum(e, axis=0, keepdims=True)
    lb = jnp.zeros((1, C_WIDTH), F32)
    for j in range(1, layer + 1):
        lb = lb + probs[j:j + 1, :]

    w = w_ref[...]
    lvl = lvl_ref[...]
    row = lax.broadcasted_iota(jnp.int32, (n, C_WIDTH), 0)
    ng = ng_ref[...]
    nl = len(HGRN_LEVELS)
    for ch in range(q0_ref.shape[0] // n):
        rs = slice(ch * n, (ch + 1) * n)
        both = lambda a_ref, b_ref: jnp.concatenate([a_ref[rs, :], b_ref[rs, :]], axis=1)
        z = both(f0_ref, f1_ref)
        q = both(q0_ref, q1_ref)
        f = lb + (1.0 - lb) * jax.nn.sigmoid(z)
        logf = jnp.log(jnp.maximum(f, F_MIN))
        kk = (1.0 - lb) * jax.nn.sigmoid(-z)
        vb = both(i0_ref, i1_ref).astype(BF16)
        l1 = logf.astype(BF16)
        l2 = (logf - l1.astype(F32)).astype(BF16)
        expo = _dot(w, l1) + _dot(w, l2)
        xs = []
        for li, h in enumerate(HGRN_LEVELS):
            upper = (row & h) != 0
            xs.append((jnp.where(upper, q, kk) * jnp.exp(expo[li * n:(li + 1) * n])).astype(BF16))
        x1 = jnp.where((row & 1) != 0, q * jnp.maximum(f, F_MIN), kk).astype(BF16)
        q_in = (q * jnp.exp(expo[nl * n:(nl + 1) * n])).astype(BF16)
        k_out = (kk * jnp.exp(expo[(nl + 1) * n:(nl + 2) * n])).astype(BF16)
        carry = jnp.exp(expo[(nl + 1) * n - 1:(nl + 1) * n])
        qb, kb = q.astype(BF16), kk.astype(BF16)
        outs = []
        for hd in range(C_HEADS):
            cs = slice(hd * C_DIM, (hd + 1) * C_DIM)
            scores = jnp.where(lvl == -1, _dot_nt(qb[:, cs], kb[:, cs]), 0.0)
            scores = scores + jnp.where(lvl == 0, _dot_nt(x1[:, cs], x1[:, cs]), 0.0)
            for li, h in enumerate(HGRN_LEVELS):
                scores = scores + jnp.where(lvl == int(np.log2(h)), _dot_nt(xs[li][:, cs], xs[li][:, cs]), 0.0)
            st = st_ref[hd]
            o = _dot(scores.astype(BF16), vb[:, cs]) + _dot_nt(q_in[:, cs], st.astype(BF16))
            st_ref[hd] = st * carry[:, cs] + _dot_tn(vb[:, cs], k_out[:, cs])
            outs.append(o * lax.rsqrt(jnp.mean(o * o, axis=-1, keepdims=True) + LN_EPS))
        o = jnp.concatenate(outs, axis=1) * ng
        o_ref[rs, :] = (o * jax.nn.sigmoid(both(g0_ref, g1_ref))).astype(o_ref.dtype)


def _hgrn(z, lb_logits, norm_g, layer, batch, seq):
    t = z.shape[0]
    rows = min(1024, seq)
    per_seq = seq // rows
    half = C_WIDTH // 2
    w, lvl = _hgrn_tables()
    col = lambda c0, j: pl.BlockSpec((rows, half), lambda b, c: (b * per_seq + c, c0 // half + j))
    const = lambda shape: pl.BlockSpec(shape, lambda b, c: (0,) * len(shape))
    return pl.pallas_call(
        functools.partial(_hgrn_kernel, layer=layer),
        grid=(batch, per_seq),
        in_specs=[const(lb_logits.shape), const(norm_g.shape), const(w.shape), const(lvl.shape)]
        + [col(c0, j) for c0 in (COL_QC, COL_FC, COL_IC, COL_GC) for j in (0, 1)],
        out_specs=pl.BlockSpec((rows, C_WIDTH), lambda b, c: (b * per_seq + c, 0)),
        out_shape=jax.ShapeDtypeStruct((t, C_WIDTH), BF16),
        scratch_shapes=[pltpu.VMEM((C_HEADS, C_DIM, C_DIM), F32)],
        compiler_params=_cparams("parallel", "arbitrary"),
        name="hgrn2",
    )(lb_logits, norm_g, jnp.asarray(w, BF16), jnp.asarray(lvl), *([z] * 8))


def _merge_kernel(x_ref, oa_ref, ob_ref, oc_ref, d0_ref, d1_ref, d2_ref, l0_ref, l1_ref, l2_ref, gt_ref,
                  wa_ref, wb_ref, wc_ref, wd_ref, wo_ref, g_ref, b_ref, o_ref, obf_ref, y_ref, *, alpha):
    d = x_ref.shape[1]
    i = pl.program_id(0)
    nt = pl.num_programs(0) - 1

    def finish_previous():
        o = _layer_norm(y_ref[...], g_ref[...], b_ref[...])
        o_ref[...] = o
        obf_ref[...] = o.astype(BF16)

    @pl.when(i == 0)
    def _():
        y_ref[...] = jnp.zeros_like(y_ref)

    @pl.when(i < nt)
    def _():
        finish_previous()
        l0, l1, l2 = l0_ref[...], l1_ref[...], l2_ref[...]
        m = jnp.maximum(jnp.maximum(l0, l1), l2)
        e0, e1, e2 = jnp.exp(l0 - m), jnp.exp(l1 - m), jnp.exp(l2 - m)
        inv = 1.0 / (e0 + e1 + e2)
        od = (e0 * inv) * d0_ref[...] + (e1 * inv) * d1_ref[...] + (e2 * inv) * d2_ref[...]
        merged = gt_ref[:, 0:d].astype(F32) * _dot(oa_ref[...], wa_ref[...])
        merged = merged + gt_ref[:, d:2 * d].astype(F32) * _dot(ob_ref[...], wb_ref[...])
        merged = merged + gt_ref[:, 2 * d:3 * d].astype(F32) * _dot(oc_ref[...], wc_ref[...])
        merged = merged + gt_ref[:, 3 * d:4 * d].astype(F32) * _dot(od.astype(BF16), wd_ref[...])
        y_ref[...] = alpha * x_ref[...] + _dot(merged.astype(BF16), wo_ref[...])

    @pl.when(i == nt)
    def _():
        finish_previous()


def _merge(x, oa, ob, oc, ods, lses, gates, wa, wb, wc, wd, wo, layer, ln_g, ln_b, alpha):
    t, d = x.shape
    tm = min(256, t)
    nt = t // tm
    row = lambda width: pl.BlockSpec((tm, width), lambda i: (jnp.minimum(i, nt - 1), 0))
    lag = pl.BlockSpec((tm, d), lambda i: (jnp.maximum(i - 1, 0), 0))
    wspec = lambda k: pl.BlockSpec((None, k, d), lambda i: (layer, 0, 0), pipeline_mode=pl.Buffered(1))
    vec = pl.BlockSpec((1, d), lambda i: (0, 0))
    return pl.pallas_call(
        functools.partial(_merge_kernel, alpha=alpha),
        grid=(nt + 1,),
        in_specs=[row(d), row(A_WIDTH), row(B_WIDTH), row(C_WIDTH)] + [row(D_GROUP_WIDTH)] * 6
        + [row(N_BRANCHES * d), wspec(A_WIDTH), wspec(B_WIDTH), wspec(C_WIDTH), wspec(D_GROUP_WIDTH), wspec(d),
           vec, vec],
        out_specs=[lag, lag],
        out_shape=[jax.ShapeDtypeStruct((t, d), F32), jax.ShapeDtypeStruct((t, d), BF16)],
        scratch_shapes=[pltpu.VMEM((tm, d), F32)],
        compiler_params=_cparams("arbitrary"),
        name="merge_out_ln",
    )(x, oa, ob, oc, *ods, *lses, gates, wa, wb, wc, wd, wo, ln_g, ln_b)


def _ple_kernel(xb_ref, p_ref, wg_ref, wp_ref, o_ref):
    gate = jax.nn.sigmoid(_dot(xb_ref[...], wg_ref[...]))
    o_ref[...] = gate * _dot(p_ref[...].astype(BF16), wp_ref[...])


def _ple(xb, p, wg, wp, layer):
    t, d = xb.shape
    pd = p.shape[-1]
    tm = min(1024, t)
    tn = min(1024, d)
    return pl.pallas_call(
        _ple_kernel,
        grid=(d // tn, t // tm),
        in_specs=[
            pl.BlockSpec((tm, d), lambda n, m: (m, 0)),
            pl.BlockSpec((None, tm, pd), lambda n, m: (layer, m, 0)),
            pl.BlockSpec((None, d, tn), lambda n, m: (layer, 0, n)),
            pl.BlockSpec((None, pd, tn), lambda n, m: (layer, 0, n)),
        ],
        out_specs=pl.BlockSpec((tm, tn), lambda n, m: (m, n)),
        out_shape=jax.ShapeDtypeStruct((t, d), F32),
        compiler_params=_cparams("parallel", "arbitrary"),
        name="ple_gate",
    )(xb, p, wg, wp)


def _alibi_slopes(n):
    return jnp.exp2(-8.0 * jnp.arange(1, n + 1, dtype=F32) / n)


def kernel(x, p, ln_g, ln_b, ffn_w_gate, ffn_w_up, ffn_w_down, w_in, attn_sinks, gmlp_ln_g, gmlp_ln_b, gmlp_w_s,
           gmlp_b_s, hgrn_lb_logits, hgrn_norm_g, w_br_a, w_br_b, w_br_c, w_br_d, w_out, ple_w_proj, ple_w_gate):
    batch, seq, d = x.shape
    depth = ln_g.shape[0]
    t = batch * seq
    alpha = (2.0 * depth) ** 0.25
    assert seq % (BLOCK * D_PATTERNS[-1][1]) == 0, "the widest dilation needs whole 128-row residue blocks"
    assert w_in.shape[-1] == MIX_WIDTH + N_BRANCHES * d
    bf =lambda a: a.astype(BF16)
    wg, wu, wd = bf(ffn_w_gate), bf(ffn_w_up), bf(ffn_w_down)
    w_in_b = bf(w_in)
    wa, wb, wc, wdd, wo = bf(w_br_a), bf(w_br_b), bf(w_br_c), bf(w_br_d), bf(w_out)
    wpp, wpg = bf(ple_w_proj), bf(ple_w_gate)
    p2 = p.reshape(depth, t, p.shape[-1])
    slopes = _alibi_slopes(N_SOFTMAX_HEADS)
    no_sinks = jnp.zeros((D_HEADS,), F32)

    xf = x.reshape(t, d)
    for i in range(depth):
        xf, xb = _ffn(xf, wg, wu, wd, i, 0, ln_g[i, 0][None], ln_b[i, 0][None], None, alpha)
        z = _proj(xb, w_in_b, i, 0, MIX_WIDTH, F32, False)
        gates = _proj(xb, w_in_b, i, MIX_WIDTH, N_BRANCHES * d, BF16, True)
        oa = _band_attn(z, slopes[:A_Q_HEADS], attn_sinks[i], seq=seq, q_col=COL_QA, k_col=COL_KA, v_col=COL_VA,
                        n_pairs=A_Q_HEADS // 2, dil=1, max_dist=A_WINDOW - 1, gqa=True, use_sinks=True,
                        want_lse=False, out_dtype=BF16)[0]
        ob = _gmlp(z, gmlp_ln_g[i][None], gmlp_ln_b[i][None], gmlp_w_s[i], gmlp_b_s[i].T)
        oc = _hgrn(z, hgrn_lb_logits, hgrn_norm_g[i][None], i, batch, seq)
        ods, lses = [], []
        for g, (window, dil) in enumerate(D_PATTERNS):
            h0 = A_Q_HEADS + g * D_HEADS
            od, lse = _band_attn(z, slopes[h0:h0 + D_HEADS], no_sinks, seq=seq, q_col=COL_QD + g * D_GROUP_WIDTH,
                                 k_col=COL_KD + g * D_GROUP_WIDTH, v_col=COL_VD + g * D_GROUP_WIDTH,
                                 n_pairs=D_HEADS // 2, dil=dil, max_dist=window // dil, gqa=False, use_sinks=False,
                                 want_lse=True, out_dtype=F32)
            ods.append(od)
            lses.append(lse)
        xf, xb = _merge(xf, oa, ob, oc, ods, lses, gates, wa, wb, wc, wdd, wo, i, ln_g[i, 1][None], ln_b[i, 1][None],
                        alpha)
        ple = _ple(xb, p2, wpg, wpp, i)
        xf, xb = _ffn(xf, wg, wu, wd, i, 1, ln_g[i, 2][None], ln_b[i, 2][None], ple, alpha)
    return xf.reshape(batch, seq, d)
```

```python
import functools

import numpy as np

import jax
import jax.numpy as jnp
from jax import lax
from jax.experimental import pallas as pl
from jax.experimental.pallas import tpu as pltpu

F32 = jnp.float32
BF16 = jnp.bfloat16

LANES = 128
VMEM_LIMIT_BYTES = 56 * 1024 * 1024

HEAD_DIM = 64
BLOCK = 128
LN_EPS = 1e-5
NEG_BIG = -1e30
F_MIN = 1e-6
A_Q_HEADS = 8
A_KV_HEADS = 2
A_WINDOW = 128
B_GROUPS = 4
B_CHUNK = 128
C_HEADS = 4
C_DIM = 128
D_PATTERNS = ((128, 1), (512, 4), (2048, 16))
D_HEADS = 4
N_SOFTMAX_HEADS = A_Q_HEADS + len(D_PATTERNS) * D_HEADS
N_BRANCHES = 4
A_WIDTH = A_Q_HEADS * HEAD_DIM
A_KV_WIDTH = A_KV_HEADS * HEAD_DIM
B_WIDTH = B_GROUPS * B_CHUNK
C_WIDTH = C_HEADS * C_DIM
D_GROUP_WIDTH = D_HEADS * HEAD_DIM
D_WIDTH = len(D_PATTERNS) * D_GROUP_WIDTH
COL_QA = 0
COL_KA = COL_QA + A_WIDTH
COL_VA = COL_KA + A_KV_WIDTH
COL_UB = COL_VA + A_KV_WIDTH
COL_VB = COL_UB + B_WIDTH
COL_QC = COL_VB + B_WIDTH
COL_FC = COL_QC + C_WIDTH
COL_IC = COL_FC + C_WIDTH
COL_GC = COL_IC + C_WIDTH
COL_QD = COL_GC + C_WIDTH
COL_KD = COL_QD + D_WIDTH
COL_VD = COL_KD + D_WIDTH
MIX_WIDTH = COL_VD + D_WIDTH

EPILOGUE_PANEL = 256
WEIGHT_DMA_PRIORITY = 1
ATTN_UNROLL = 8
HGRN_CHUNK = 128
HGRN_LEVELS = (64, 32, 16, 8, 4, 2)


def _cparams(*sem):
    return pltpu.CompilerParams(dimension_semantics=sem, vmem_limit_bytes=VMEM_LIMIT_BYTES)


def _layer_norm(v, g, b):
    mu = jnp.mean(v, axis=-1, keepdims=True)
    d = v - mu
    var = jnp.mean(d * d, axis=-1, keepdims=True)
    return d * lax.rsqrt(var + LN_EPS) * g + b


def _dot(a, b):
    return jnp.dot(a, b, preferred_element_type=F32)


def _dot_nt(a, b):
    return lax.dot_general(a, b, (((1,), (1,)), ((), ())), preferred_element_type=F32)


def _dot_tn(a, b):
    return lax.dot_general(a, b, (((0,), (0,)), ((), ())), preferred_element_type=F32)


def _ffn_kernel(x_ref, wg_hbm, wu_hbm, wd_hbm, g_ref, b_ref, *rest, alpha, has_extra, layer, which, nt, nf, tf):
    if has_extra:
        e_ref, o_ref, ob_ref, xb_ref, acc_ref, wgu_buf, wd_buf, sem = rest
    else:
        o_ref, ob_ref, xb_ref, acc_ref, wgu_buf, wd_buf, sem = rest
    i = pl.program_id(0)
    base = i * nf

    def copies(pos):
        slot = pos % 2
        off = pl.multiple_of((pos % nf) * tf, tf)
        gate_dst = wgu_buf.at[slot, :, pl.ds(0, tf)]
        up_dst = wgu_buf.at[slot, :, pl.ds(tf, tf)]
        return (
            pltpu.make_async_copy(wg_hbm.at[layer, which, :, pl.ds(off, tf)], gate_dst, sem.at[0, slot]),
            pltpu.make_async_copy(wu_hbm.at[layer, which, :, pl.ds(off, tf)], up_dst, sem.at[1, slot]),
            pltpu.make_async_copy(wd_hbm.at[layer, which, pl.ds(off, tf), :], wd_buf.at[slot], sem.at[2, slot]),
        )

    def contribution(pos):
        for c in copies(pos):
            c.wait()

        @pl.when(pos + 1 < nt * nf)
        def _():
            for c in copies(pos + 1):
                c.start(priority=WEIGHT_DMA_PRIORITY)

        slot = pos % 2
        gu = _dot(xb_ref[...], wgu_buf[slot])
        g, u = gu[:, :tf], gu[:, tf:]
        h = (g * jax.nn.sigmoid(g) * u).astype(BF16)
        return _dot(h, wd_buf[slot])

    def finish_previous():
        y = 0.5 * acc_ref[...]
        if has_extra:
            y = y + e_ref[...]
        o = _layer_norm(y, g_ref[...], b_ref[...])
        o_ref[...] = o
        ob_ref[...] = o.astype(BF16)

    @pl.when(i == 0)
    def _():
        acc_ref[...] = jnp.zeros_like(acc_ref)
        for c in copies(0):
            c.start(priority=WEIGHT_DMA_PRIORITY)

    @pl.when(i < nt)
    def _():
        finish_previous()
        xb_ref[...] = x_ref[...].astype(BF16)
        c = contribution(base)
        acc_ref[...] = (2.0 * alpha) * x_ref[...] + c

        def chunk(f, carry):
            c = contribution(base + f)
            acc_ref[...] = acc_ref[...] + c
            return carry

        lax.fori_loop(1, nf, chunk, 0)

    @pl.when(i == nt)
    def _():
        finish_previous()


def _ffn(x, wg, wu, wd, layer, which, ln_g, ln_b, extra, alpha, tm=512, tf=512):
    t, d = x.shape
    ff = wg.shape[-1]
    tm = min(tm, t)
    tf = min(tf, ff)
    nt, nf = t // tm, ff // tf
    cur = pl.BlockSpec((tm, d), lambda i: (jnp.minimum(i, nt - 1), 0))
    lag = pl.BlockSpec((tm, d), lambda i: (jnp.maximum(i - 1, 0), 0))
    vec = pl.BlockSpec((1, d), lambda i: (0, 0))
    hbm = pl.BlockSpec(memory_space=pl.ANY)
    in_specs = [cur, hbm, hbm, hbm, vec, vec]
    args = [x, wg, wu, wd, ln_g, ln_b]
    if extra is not None:
        in_specs.append(lag)
        args.append(extra)
    return pl.pallas_call(
        functools.partial(_ffn_kernel, alpha=alpha, has_extra=extra is not None, layer=layer, which=which, nt=nt,
                          nf=nf, tf=tf),
        grid=(nt + 1,),
        in_specs=in_specs,
        out_specs=[lag, lag],
        out_shape=[jax.ShapeDtypeStruct((t, d), F32), jax.ShapeDtypeStruct((t, d), BF16)],
        scratch_shapes=[pltpu.VMEM((tm, d), BF16), pltpu.VMEM((tm, d), F32), pltpu.VMEM((2, d, 2 * tf), BF16),
                        pltpu.VMEM((2, tf, d), BF16), pltpu.SemaphoreType.DMA((3, 2))],
        compiler_params=_cparams("arbitrary"),
        name="ffn_ln",
    )(*args)


def _proj_kernel(x_ref, w_ref, o_ref, *, act):
    z = _dot(x_ref[...], w_ref[...])
    if act:
        z = jax.nn.sigmoid(z)
    o_ref[...] = z.astype(o_ref.dtype)


def _proj(xb, w, layer, col0, width, out_dtype, act):
    t, d = xb.shape
    tm = min(2048, t)
    tn = min(1024, width)
    off = col0 // tn
    return pl.pallas_call(
        functools.partial(_proj_kernel, act=act),
        grid=(width // tn, t // tm),
        in_specs=[
            pl.BlockSpec((tm, d), lambda n, m: (m, 0)),
            pl.BlockSpec((None, d, tn), lambda n, m: (layer, 0, n + off)),
        ],
        out_specs=pl.BlockSpec((tm, tn), lambda n, m: (m, n)),
        out_shape=jax.ShapeDtypeStruct((t, width), out_dtype),
        compiler_params=_cparams("parallel", "arbitrary"),
        name="in_proj_gates" if act else "in_proj_mix",
    )(xb, w)


def _band_attn_kernel(slope_ref, sink_ref, q_ref, k_ref, v_ref, kp_ref, vp_ref, *rest,
                      dil, nblk, max_dist, sb_per_seq, gqa, use_sinks, want_lse):
    if want_lse:
        o_ref, lse_ref, kbuf, vbuf, bias_ref = rest
    else:
        o_ref, kbuf, vbuf, bias_ref = rest
        lse_ref = None
    i = pl.program_id(0)
    p = pl.program_id(1)
    bw = BLOCK * dil
    first_in_seq = (i % sb_per_seq) == 0
    lane = lax.broadcasted_iota(jnp.int32, (1, LANES), 1)
    low = lane < HEAD_DIM

    def stage(cur_ref, prev_ref, buf):
        cur = cur_ref[...]
        prev = prev_ref[...]
        if gqa:
            kv_first = (p // 2) == 0

            def dup(a):
                r = pltpu.roll(a, HEAD_DIM, 1)
                return jnp.where(kv_first, jnp.where(low, a, r), jnp.where(low, r, a))

            cur, prev = dup(cur), dup(prev)
        buf[0:bw, :] = prev.astype(buf.dtype)
        buf[bw:, :] = cur.astype(buf.dtype)

    if gqa:
        @pl.when(p % 2 == 0)
        def _():
            stage(k_ref, kp_ref, kbuf)
            stage(v_ref, vp_ref, vbuf)
    else:
        stage(k_ref, kp_ref, kbuf)
        stage(v_ref, vp_ref, vbuf)

    q_off = lax.broadcasted_iota(jnp.int32, (2 * BLOCK, 2 * BLOCK), 0) % BLOCK
    k_off = lax.broadcasted_iota(jnp.int32, (2 * BLOCK, 2 * BLOCK), 1)
    dist = q_off + BLOCK - k_off
    in_band = (dist >= 0) & (dist <= max_dist)
    is_cur = k_off >= BLOCK
    dist_f = (jnp.maximum(dist, 0) * dil).astype(F32)
    top = lax.broadcasted_iota(jnp.int32, (2 * BLOCK, 1), 0) < BLOCK
    neg_slope = jnp.where(top, -slope_ref[2 * p], -slope_ref[2 * p + 1])
    bias = jnp.where(in_band, neg_slope * dist_f, NEG_BIG)
    bias_ref[0] = bias
    bias_ref[1] = jnp.where(is_cur, bias, NEG_BIG)
    if use_sinks:
        sink = jnp.where(top, sink_ref[2 * p], sink_ref[2 * p + 1])
    ones = jnp.ones((2 * BLOCK, LANES), BF16)

    def body(it, carry):
        j = it // dil
        c = it % dil
        row0 = j * bw + c
        if dil == 1:
            row0 = pl.multiple_of(row0, BLOCK)
            qs = pl.ds(row0, BLOCK)
            ks = pl.ds(row0, 2 * BLOCK)
        else:
            qs = pl.ds(row0, BLOCK, stride=dil)
            ks = pl.ds(row0, 2 * BLOCK, stride=dil)
        qc = q_ref[qs, :] * (HEAD_DIM ** -0.5)
        kk = kbuf[ks, :].astype(BF16)
        vv = vbuf[ks, :].astype(BF16)
        q2 = jnp.concatenate([jnp.where(low, qc, 0.0), jnp.where(low, 0.0, qc)], axis=0).astype(BF16)
        tbl = bias_ref[jnp.logical_and(j == 0, first_in_seq).astype(jnp.int32)]
        s = jnp.where(tbl > 0.5 * NEG_BIG, _dot_nt(q2, kk) + tbl, NEG_BIG)
        m = jnp.max(s, axis=-1, keepdims=True)
        if use_sinks:
            m = jnp.maximum(m, sink)
        prb = jnp.exp(s - m).astype(BF16)
        ov = _dot(prb, jnp.concatenate([vv, ones], axis=1))
        den = ov[:, LANES:]
        if use_sinks:
            den = den + jnp.exp(sink - m)
        on = ov[:, :LANES] / den
        o_ref[qs, :] = jnp.where(low, on[0:BLOCK], on[BLOCK:]).astype(o_ref.dtype)
        if want_lse:
            lse = m + jnp.log(den)
            lse_ref[qs, :] = jnp.where(low, lse[0:BLOCK], lse[BLOCK:])
        return carry

    lax.fori_loop(0, nblk * dil, body, 0, unroll=ATTN_UNROLL)


def _band_attn(z, slopes, sinks, *, seq, q_col, k_col, v_col, n_pairs, dil, max_dist, gqa, use_sinks,
               want_lse, out_dtype):
    t = z.shape[0]
    bw = BLOCK * dil
    sb = min(max(2 * bw, 2048), seq)
    nblk = sb // bw
    qb, kb, vb = q_col // LANES, k_col // LANES, v_col // LANES
    if gqa:
        kv_map = lambda i, p: (i, kb)
        vv_map = lambda i, p: (i, vb)
        kp_map = lambda i, p: (jnp.maximum(i * nblk - 1, 0), kb)
        vp_map = lambda i, p: (jnp.maximum(i * nblk - 1, 0), vb)
    else:
        kv_map = lambda i, p: (i, kb + p)
        vv_map = lambda i, p: (i, vb + p)
        kp_map = lambda i, p: (jnp.maximum(i * nblk - 1, 0), kb + p)
        vp_map = lambda i, p: (jnp.maximum(i * nblk - 1, 0), vb + p)
    smem = pl.BlockSpec(memory_space=pltpu.SMEM)
    out_spec = pl.BlockSpec((sb, LANES), lambda i, p: (i, p))
    out_specs = [out_spec]
    out_shape = [jax.ShapeDtypeStruct((t, n_pairs * LANES), out_dtype)]
    if want_lse:
        out_specs.append(out_spec)
        out_shape.append(jax.ShapeDtypeStruct((t, n_pairs * LANES), F32))
    return pl.pallas_call(
        functools.partial(_band_attn_kernel, dil=dil, nblk=nblk, max_dist=max_dist, sb_per_seq=seq // sb,
                          gqa=gqa, use_sinks=use_sinks, want_lse=want_lse),
        grid=(t // sb, n_pairs),
        in_specs=[
            smem,
            smem,
            pl.BlockSpec((sb, LANES), lambda i, p: (i, qb + p)),
            pl.BlockSpec((sb, LANES), kv_map),
            pl.BlockSpec((sb, LANES), vv_map),
            pl.BlockSpec((bw, LANES), kp_map),
            pl.BlockSpec((bw, LANES), vp_map),
        ],
        out_specs=out_specs,
        out_shape=out_shape,
        scratch_shapes=[pltpu.VMEM((bw + sb, LANES), BF16 if dil == 1 else F32)] * 2
        + [pltpu.VMEM((2, 2 * BLOCK, 2 * BLOCK), F32)],
        compiler_params=_cparams("parallel", "arbitrary"),
        name=f"band_attn_d{dil}" + ("_gqa" if gqa else ""),
    )(slopes, sinks, z, z, z, z, z)


def _gelu(x):
    return 0.5 * x * (1.0 + lax.erf(x * np.float32(np.sqrt(0.5))))


def _gmlp_kernel(u0_ref, u1_ref, v0_ref, v1_ref, g_ref, b_ref, ws_ref, bs_ref, o_ref):
    rows = u0_ref.shape[0]
    u = _gelu(jnp.concatenate([u0_ref[...], u1_ref[...]], axis=1))
    v = _gelu(jnp.concatenate([v0_ref[...], v1_ref[...]], axis=1))
    v = _layer_norm(v, g_ref[...], b_ref[...]).astype(BF16)
    t_idx = lax.broadcasted_iota(jnp.int32, (B_CHUNK, B_CHUNK), 0)
    s_idx = lax.broadcasted_iota(jnp.int32, (B_CHUNK, B_CHUNK), 1)
    causal = t_idx >= s_idx
    for grp in range(B_GROUPS):
        w = jnp.where(causal, ws_ref[grp], 0.0).astype(BF16)
        bias = bs_ref[:, grp:grp + 1]
        cols = slice(grp * B_CHUNK, (grp + 1) * B_CHUNK)
        for ch in range(rows // B_CHUNK):
            rs = slice(ch * B_CHUNK, (ch + 1) * B_CHUNK)
            mixed = _dot(w, v[rs, cols]) + bias
            o_ref[rs, cols] = (u[rs, cols] * mixed).astype(o_ref.dtype)


def _gmlp(z, ln_g, ln_b, w_s, b_s_t):
    t = z.shape[0]
    rows = min(2048, t)
    half = B_WIDTH // 2
    ub, vb = COL_UB // half, COL_VB // half
    col = lambda j: pl.BlockSpec((rows, half), lambda i: (i, j))
    vec = pl.BlockSpec((1, B_WIDTH), lambda i: (0, 0))
    return pl.pallas_call(
        _gmlp_kernel,
        grid=(t // rows,),
        in_specs=[col(ub), col(ub + 1), col(vb), col(vb + 1), vec, vec,
                  pl.BlockSpec((B_GROUPS, B_CHUNK, B_CHUNK), lambda i: (0, 0, 0)),
                  pl.BlockSpec((B_CHUNK, B_GROUPS), lambda i: (0, 0))],
        out_specs=pl.BlockSpec((rows, B_WIDTH), lambda i: (i, 0)),
        out_shape=jax.ShapeDtypeStruct((t, B_WIDTH), BF16),
        compiler_params=_cparams("parallel"),
        name="spatial_gating",
    )(z, z, z, z, ln_g, ln_b, w_s, b_s_t)


def _hgrn_tables():
    n = HGRN_CHUNK
    t = np.arange(n)[:, None]
    i = np.arange(n)[None, :]
    tabs = []
    for h in HGRN_LEVELS:
        start = (t // h) * h
        upper = ((t // h) % 2) == 1
        q_tab = (i >= start) & (i <= t)
        k_tab = (i > t) & (i < start + h)
        tabs.append(np.where(upper, q_tab, k_tab))
    tabs.append(i <= t)
    tabs.append(i > t)
    w = np.concatenate(tabs, axis=0).astype(np.float32)
    s = np.arange(n)[None, :]
    x = np.bitwise_xor(t, s)
    level = np.where(s < t, np.floor(np.log2(np.maximum(x, 1))).astype(np.int32), np.where(s == t, -1, -2))
    return w, level.astype(np.int32)


def _hgrn_kernel(lbl_ref, ng_ref, w_ref, lvl_ref, q0_ref, q1_ref, f0_ref, f1_ref, i0_ref, i1_ref, g0_ref, g1_ref,
                 o_ref, st_ref, *, layer):
    n = HGRN_CHUNK

    @pl.when(pl.program_id(1) == 0)
    def _():
        st_ref[...] = jnp.zeros_like(st_ref)

    logits = lbl_ref[...]
    e = jnp.exp(logits - jnp.max(logits, axis=0, keepdims=True))
    probs = e / jnp.sum(e, axis=0, keepdims=True)
    lb = jnp.zeros((1, C_WIDTH), F32)
    for j in range(1, layer + 1):
        lb = lb + probs[j:j + 1, :]

    w = w_ref[...]
    lvl = lvl_ref[...]
    row = lax.broadcasted_iota(jnp.int32, (n, C_WIDTH), 0)
    ng = ng_ref[...]
    nl = len(HGRN_LEVELS)
    for ch in range(q0_ref.shape[0] // n):
        rs = slice(ch * n, (ch + 1) * n)
        both = lambda a_ref, b_ref: jnp.concatenate([a_ref[rs, :], b_ref[rs, :]], axis=1)
        z = both(f0_ref, f1_ref)
        q = both(q0_ref, q1_ref)
        f = lb + (1.0 - lb) * jax.nn.sigmoid(z)
        logf = jnp.log(jnp.maximum(f, F_MIN))
        kk = (1.0 - lb) * jax.nn.sigmoid(-z)
        vb = both(i0_ref, i1_ref).astype(BF16)
        l1 = logf.astype(BF16)
        l2 = (logf - l1.astype(F32)).astype(BF16)
        expo = _dot(w, l1) + _dot(w, l2)
        xs = []
        for li, h in enumerate(HGRN_LEVELS):
            upper = (row & h) != 0
            xs.append((jnp.where(upper, q, kk) * jnp.exp(expo[li * n:(li + 1) * n])).astype(BF16))
        x1 = jnp.where((row & 1) != 0, q * jnp.maximum(f, F_MIN), kk).astype(BF16)
        q_in = (q * jnp.exp(expo[nl * n:(nl + 1) * n])).astype(BF16)
        k_out = (kk * jnp.exp(expo[(nl + 1) * n:(nl + 2) * n])).astype(BF16)
        carry = jnp.exp(expo[(nl + 1) * n - 1:(nl + 1) * n])
        qb, kb = q.astype(BF16), kk.astype(BF16)
        outs = []
        for hd in range(C_HEADS):
            cs = slice(hd * C_DIM, (hd + 1) * C_DIM)
            scores = jnp.where(lvl == -1, _dot_nt(qb[:, cs], kb[:, cs]), 0.0)
            scores = scores + jnp.where(lvl == 0, _dot_nt(x1[:, cs], x1[:, cs]), 0.0)
            for li, h in enumerate(HGRN_LEVELS):
                scores = scores + jnp.where(lvl == int(np.log2(h)), _dot_nt(xs[li][:, cs], xs[li][:, cs]), 0.0)
            st = st_ref[hd]
            o = _dot(scores.astype(BF16), vb[:, cs]) + _dot_nt(q_in[:, cs], st.astype(BF16))
            st_ref[hd] = st * carry[:, cs] + _dot_tn(vb[:, cs], k_out[:, cs])
            outs.append(o * lax.rsqrt(jnp.mean(o * o, axis=-1, keepdims=True) + LN_EPS))
        o = jnp.concatenate(outs, axis=1) * ng
        o_ref[rs, :] = (o * jax.nn.sigmoid(both(g0_ref, g1_ref))).astype(o_ref.dtype)


def _hgrn(z, lb_logits, norm_g, layer, batch, seq):
    t = z.shape[0]
    rows = min(1024, seq)
    per_seq = seq // rows
    half = C_WIDTH // 2
    w, lvl = _hgrn_tables()
    col = lambda c0, j: pl.BlockSpec((rows, half), lambda b, c: (b * per_seq + c, c0 // half + j))
    const = lambda shape: pl.BlockSpec(shape, lambda b, c: (0,) * len(shape))
    return pl.pallas_call(
        functools.partial(_hgrn_kernel, layer=layer),
        grid=(batch, per_seq),
        in_specs=[const(lb_logits.shape), const(norm_g.shape), const(w.shape), const(lvl.shape)]
        + [col(c0, j) for c0 in (COL_QC, COL_FC, COL_IC, COL_GC) for j in (0, 1)],
        out_specs=pl.BlockSpec((rows, C_WIDTH), lambda b, c: (b * per_seq + c, 0)),
        out_shape=jax.ShapeDtypeStruct((t, C_WIDTH), BF16),
        scratch_shapes=[pltpu.VMEM((C_HEADS, C_DIM, C_DIM), F32)],
        compiler_params=_cparams("parallel", "arbitrary"),
        name="hgrn2",
    )(lb_logits, norm_g, jnp.asarray(w, BF16), jnp.asarray(lvl), *([z] * 8))


def _merge_kernel(x_ref, oa_ref, ob_ref, oc_ref, d0_ref, d1_ref, d2_ref, l0_ref, l1_ref, l2_ref, gt_ref,
                  wa_ref, wb_ref, wc_ref, wd_ref, wo_ref, g_ref, b_ref, o_ref, obf_ref, y_ref, *, alpha):
    d = x_ref.shape[1]
    i = pl.program_id(0)
    nt = pl.num_programs(0) - 1

    def finish_previous():
        o = _layer_norm(y_ref[...], g_ref[...], b_ref[...])
        o_ref[...] = o
        obf_ref[...] = o.astype(BF16)

    @pl.when(i == 0)
    def _():
        y_ref[...] = jnp.zeros_like(y_ref)

    @pl.when(i < nt)
    def _():
        finish_previous()
        l0, l1, l2 = l0_ref[...], l1_ref[...], l2_ref[...]
        m = jnp.maximum(jnp.maximum(l0, l1), l2)
        e0, e1, e2 = jnp.exp(l0 - m), jnp.exp(l1 - m), jnp.exp(l2 - m)
        inv = 1.0 / (e0 + e1 + e2)
        od = (e0 * inv) * d0_ref[...] + (e1 * inv) * d1_ref[...] + (e2 * inv) * d2_ref[...]
        merged = gt_ref[:, 0:d].astype(F32) * _dot(oa_ref[...], wa_ref[...])
        merged = merged + gt_ref[:, d:2 * d].astype(F32) * _dot(ob_ref[...], wb_ref[...])
        merged = merged + gt_ref[:, 2 * d:3 * d].astype(F32) * _dot(oc_ref[...], wc_ref[...])
        merged = merged + gt_ref[:, 3 * d:4 * d].astype(F32) * _dot(od.astype(BF16), wd_ref[...])
        y_ref[...] = alpha * x_ref[...] + _dot(merged.astype(BF16), wo_ref[...])

    @pl.when(i == nt)
    def _():
        finish_previous()


def _merge(x, oa, ob, oc, ods, lses, gates, wa, wb, wc, wd, wo, layer, ln_g, ln_b, alpha):
    t, d = x.shape
    tm = min(256, t)
    nt = t // tm
    row = lambda width: pl.BlockSpec((tm, width), lambda i: (jnp.minimum(i, nt - 1), 0))
    lag = pl.BlockSpec((tm, d), lambda i: (jnp.maximum(i - 1, 0), 0))
    wspec = lambda k: pl.BlockSpec((None, k, d), lambda i: (layer, 0, 0), pipeline_mode=pl.Buffered(1))
    vec = pl.BlockSpec((1, d), lambda i: (0, 0))
    return pl.pallas_call(
        functools.partial(_merge_kernel, alpha=alpha),
        grid=(nt + 1,),
        in_specs=[row(d), row(A_WIDTH), row(B_WIDTH), row(C_WIDTH)] + [row(D_GROUP_WIDTH)] * 6
        + [row(N_BRANCHES * d), wspec(A_WIDTH), wspec(B_WIDTH), wspec(C_WIDTH), wspec(D_GROUP_WIDTH), wspec(d),
           vec, vec],
        out_specs=[lag, lag],
        out_shape=[jax.ShapeDtypeStruct((t, d), F32), jax.ShapeDtypeStruct((t, d), BF16)],
        scratch_shapes=[pltpu.VMEM((tm, d), F32)],
        compiler_params=_cparams("arbitrary"),
        name="merge_out_ln",
    )(x, oa, ob, oc, *ods, *lses, gates, wa, wb, wc, wd, wo, ln_g, ln_b)


def _ple_kernel(xb_ref, p_ref, wg_ref, wp_ref, o_ref):
    pb = p_ref[...].astype(BF16)
    for c0 in range(0, o_ref.shape[1], EPILOGUE_PANEL):
        cs = slice(c0, c0 + EPILOGUE_PANEL)
        o_ref[:, cs] = jax.nn.sigmoid(_dot(xb_ref[...], wg_ref[:, cs])) * _dot(pb, wp_ref[:, cs])


def _ple(xb, p, wg, wp, layer):
    t, d = xb.shape
    pd = p.shape[-1]
    tm = min(1024, t)
    tn = min(1024, d)
    return pl.pallas_call(
        _ple_kernel,
        grid=(d // tn, t // tm),
        in_specs=[
            pl.BlockSpec((tm, d), lambda n, m: (m, 0)),
            pl.BlockSpec((None, tm, pd), lambda n, m: (layer, m, 0)),
            pl.BlockSpec((None, d, tn), lambda n, m: (layer, 0, n)),
            pl.BlockSpec((None, pd, tn), lambda n, m: (layer, 0, n)),
        ],
        out_specs=pl.BlockSpec((tm, tn), lambda n, m: (m, n)),
        out_shape=jax.ShapeDtypeStruct((t, d), F32),
        compiler_params=_cparams("parallel", "arbitrary"),
        name="ple_gate",
    )(xb, p, wg, wp)


def _alibi_slopes(n):
    return jnp.exp2(-8.0 * jnp.arange(1, n + 1, dtype=F32) / n)


def kernel(x, p, ln_g, ln_b, ffn_w_gate, ffn_w_up, ffn_w_down, w_in, attn_sinks, gmlp_ln_g, gmlp_ln_b, gmlp_w_s,
           gmlp_b_s, hgrn_lb_logits, hgrn_norm_g, w_br_a, w_br_b, w_br_c, w_br_d, w_out, ple_w_proj, ple_w_gate):
    batch, seq, d = x.shape
    depth = ln_g.shape[0]
    t = batch * seq
    alpha = (2.0 * depth) ** 0.25
    assert seq % (BLOCK * D_PATTERNS[-1][1]) == 0, "the widest dilation needs whole 128-row residue blocks"
    assert w_in.shape[-1] == MIX_WIDTH + N_BRANCHES * d
    bf =lambda a: a.astype(BF16)
    wg, wu, wd = bf(ffn_w_gate), bf(ffn_w_up), bf(ffn_w_down)
    w_in_b = bf(w_in)
    wa, wb, wc, wdd, wo = bf(w_br_a), bf(w_br_b), bf(w_br_c), bf(w_br_d), bf(w_out)
    wpp, wpg = bf(ple_w_proj), bf(ple_w_gate)
    p2 = p.reshape(depth, t, p.shape[-1])
    slopes = _alibi_slopes(N_SOFTMAX_HEADS)
    no_sinks = jnp.zeros((D_HEADS,), F32)

    xf = x.reshape(t, d)
    for i in range(depth):
        xf, xb = _ffn(xf, wg, wu, wd, i, 0, ln_g[i, 0][None], ln_b[i, 0][None], None, alpha)
        z = _proj(xb, w_in_b, i, 0, MIX_WIDTH, F32, False)
        gates = _proj(xb, w_in_b, i, MIX_WIDTH, N_BRANCHES * d, BF16, True)
        oa = _band_attn(z, slopes[:A_Q_HEADS], attn_sinks[i], seq=seq, q_col=COL_QA, k_col=COL_KA, v_col=COL_VA,
                        n_pairs=A_Q_HEADS // 2, dil=1, max_dist=A_WINDOW - 1, gqa=True, use_sinks=True,
                        want_lse=False, out_dtype=BF16)[0]
        ob = _gmlp(z, gmlp_ln_g[i][None], gmlp_ln_b[i][None], gmlp_w_s[i], gmlp_b_s[i].T)
        oc = _hgrn(z, hgrn_lb_logits, hgrn_norm_g[i][None], i, batch, seq)
        ods, lses = [], []
        for g, (window, dil) in enumerate(D_PATTERNS):
            h0 = A_Q_HEADS + g * D_HEADS
            od, lse = _band_attn(z, slopes[h0:h0 + D_HEADS], no_sinks, seq=seq, q_col=COL_QD + g * D_GROUP_WIDTH,
                                 k_col=COL_KD + g * D_GROUP_WIDTH, v_col=COL_VD + g * D_GROUP_WIDTH,
                                 n_pairs=D_HEADS // 2, dil=dil, max_dist=window // dil, gqa=False, use_sinks=False,
                                 want_lse=True, out_dtype=F32)
            ods.append(od)
            lses.append(lse)
        xf, xb = _merge(xf, oa, ob, oc, ods, lses, gates, wa, wb, wc, wdd, wo, i, ln_g[i, 1][None], ln_b[i, 1][None],
                        alpha)
        ple = _ple(xb, p2, wpg, wpp, i)
        xf, xb = _ffn(xf, wg, wu, wd, i, 1, ln_g[i, 2][None], ln_b[i, 2][None], ple, alpha)
    return xf.reshape(batch, seq, d)
```

```python
import functools

import numpy as np

import jax
import jax.numpy as jnp
from jax import lax
from jax.experimental import pallas as pl
from jax.experimental.pallas import tpu as pltpu

F32 = jnp.float32
BF16 = jnp.bfloat16

LANES = 128
VMEM_LIMIT_BYTES = 60 * 1024 * 1024

HEAD_DIM = 64
BLOCK = 128
LN_EPS = 1e-5
NEG_BIG = -1e30
F_MIN = 1e-6
A_Q_HEADS = 8
A_KV_HEADS = 2
A_WINDOW = 128
B_GROUPS = 4
B_CHUNK = 128
C_HEADS = 4
C_DIM = 128
D_PATTERNS = ((128, 1), (512, 4), (2048, 16))
D_HEADS = 4
N_SOFTMAX_HEADS = A_Q_HEADS + len(D_PATTERNS) * D_HEADS
N_BRANCHES = 4
A_WIDTH = A_Q_HEADS * HEAD_DIM
A_KV_WIDTH = A_KV_HEADS * HEAD_DIM
B_WIDTH = B_GROUPS * B_CHUNK
C_WIDTH = C_HEADS * C_DIM
D_GROUP_WIDTH = D_HEADS * HEAD_DIM
D_WIDTH = len(D_PATTERNS) * D_GROUP_WIDTH
COL_QA = 0
COL_KA = COL_QA + A_WIDTH
COL_VA = COL_KA + A_KV_WIDTH
COL_UB = COL_VA + A_KV_WIDTH
COL_VB = COL_UB + B_WIDTH
COL_QC = COL_VB + B_WIDTH
COL_FC = COL_QC + C_WIDTH
COL_IC = COL_FC + C_WIDTH
COL_GC = COL_IC + C_WIDTH
COL_QD = COL_GC + C_WIDTH
COL_KD = COL_QD + D_WIDTH
COL_VD = COL_KD + D_WIDTH
MIX_WIDTH = COL_VD + D_WIDTH

EPILOGUE_PANEL = 256
FFN_RING_SLOTS = 3
ATTN_UNROLL = 8
HGRN_CHUNK = 128
HGRN_LEVELS = (64, 32, 16, 8, 4, 2)


def _cparams(*sem):
    return pltpu.CompilerParams(dimension_semantics=sem, vmem_limit_bytes=VMEM_LIMIT_BYTES)


def _layer_norm(v, g, b):
    mu = jnp.mean(v, axis=-1, keepdims=True)
    d = v - mu
    var = jnp.mean(d * d, axis=-1, keepdims=True)
    return d * lax.rsqrt(var + LN_EPS) * g + b


def _dot(a, b):
    return jnp.dot(a, b, preferred_element_type=F32)


def _dot_nt(a, b):
    return lax.dot_general(a, b, (((1,), (1,)), ((), ())), preferred_element_type=F32)


def _dot_tn(a, b):
    return lax.dot_general(a, b, (((0,), (0,)), ((), ())), preferred_element_type=F32)


def _ffn_kernel(x_ref, wg_hbm, wu_hbm, wd_hbm, g_ref, b_ref, *rest, alpha, has_extra, layer, which, nt, nf, tf):
    ahead = FFN_RING_SLOTS - 1
    if has_extra:
        e_ref, o_ref, ob_ref, xb_ref, acc_ref, wgu_buf, wd_buf, sem = rest
    else:
        o_ref, ob_ref, xb_ref, acc_ref, wgu_buf, wd_buf, sem = rest
    i = pl.program_id(0)
    base = i * nf

    def copies(pos):
        slot = pos % FFN_RING_SLOTS
        off = (pos % nf) * tf
        if not isinstance(off, int):
            off = pl.multiple_of(off, tf)
        gate_dst = wgu_buf.at[slot, :, pl.ds(0, tf)]
        up_dst = wgu_buf.at[slot, :, pl.ds(tf, tf)]
        return (
            pltpu.make_async_copy(wg_hbm.at[layer, which, :, pl.ds(off, tf)], gate_dst, sem.at[0, slot]),
            pltpu.make_async_copy(wu_hbm.at[layer, which, :, pl.ds(off, tf)], up_dst, sem.at[1, slot]),
            pltpu.make_async_copy(wd_hbm.at[layer, which, pl.ds(off, tf), :], wd_buf.at[slot], sem.at[2, slot]),
        )

    def contribution(pos):
        for c in copies(pos):
            c.wait()

        @pl.when(pos + ahead < nt * nf)
        def _():
            for c in copies(pos + ahead):
                c.start()

        slot = pos % FFN_RING_SLOTS
        gu = _dot(xb_ref[...], wgu_buf[slot])
        g, u = gu[:, :tf], gu[:, tf:]
        h = (g * jax.nn.sigmoid(g) * u).astype(BF16)
        return _dot(h, wd_buf[slot])

    def finish_previous():
        y = 0.5 * acc_ref[...]
        if has_extra:
            y = y + e_ref[...]
        o = _layer_norm(y, g_ref[...], b_ref[...])
        o_ref[...] = o
        ob_ref[...] = o.astype(BF16)

    @pl.when(i == 0)
    def _():
        acc_ref[...] = jnp.zeros_like(acc_ref)
        for pos in range(min(ahead, nt * nf)):
            for c in copies(pos):
                c.start()

    @pl.when(i < nt)
    def _():
        finish_previous()
        xb_ref[...] = x_ref[...].astype(BF16)
        c = contribution(base)
        acc_ref[...] = (2.0 * alpha) * x_ref[...] + c

        def chunk(f, carry):
            c = contribution(base + f)
            acc_ref[...] = acc_ref[...] + c
            return carry

        lax.fori_loop(1, nf, chunk, 0)

    @pl.when(i == nt)
    def _():
        finish_previous()


def _ffn(x, wg, wu, wd, layer, which, ln_g, ln_b, extra, alpha, tm=512, tf=512):
    t, d = x.shape
    ff = wg.shape[-1]
    tm = min(tm, t)
    tf = min(tf, ff)
    nt, nf = t // tm, ff // tf
    cur = pl.BlockSpec((tm, d), lambda i: (jnp.minimum(i, nt - 1), 0))
    lag = pl.BlockSpec((tm, d), lambda i: (jnp.maximum(i - 1, 0), 0))
    vec = pl.BlockSpec((1, d), lambda i: (0, 0))
    hbm = pl.BlockSpec(memory_space=pl.ANY)
    in_specs = [cur, hbm, hbm, hbm, vec, vec]
    args = [x, wg, wu, wd, ln_g, ln_b]
    if extra is not None:
        in_specs.append(lag)
        args.append(extra)
    return pl.pallas_call(
        functools.partial(_ffn_kernel, alpha=alpha, has_extra=extra is not None, layer=layer, which=which, nt=nt,
                          nf=nf, tf=tf),
        grid=(nt + 1,),
        in_specs=in_specs,
        out_specs=[lag, lag],
        out_shape=[jax.ShapeDtypeStruct((t, d), F32), jax.ShapeDtypeStruct((t, d), BF16)],
        scratch_shapes=[pltpu.VMEM((tm, d), BF16), pltpu.VMEM((tm, d), F32),
                        pltpu.VMEM((FFN_RING_SLOTS, d, 2 * tf), BF16), pltpu.VMEM((FFN_RING_SLOTS, tf, d), BF16),
                        pltpu.SemaphoreType.DMA((3, FFN_RING_SLOTS))],
        compiler_params=_cparams("arbitrary"),
        name="ffn_ln",
    )(*args)


def _proj_kernel(x_ref, w_ref, o_ref, *, act):
    z = _dot(x_ref[...], w_ref[...])
    if act:
        z = jax.nn.sigmoid(z)
    o_ref[...] = z.astype(o_ref.dtype)


def _proj(xb, w, layer, col0, width, out_dtype, act):
    t, d = xb.shape
    tm = min(2048, t)
    tn = min(1024, width)
    off = col0 // tn
    return pl.pallas_call(
        functools.partial(_proj_kernel, act=act),
        grid=(width // tn, t // tm),
        in_specs=[
            pl.BlockSpec((tm, d), lambda n, m: (m, 0)),
            pl.BlockSpec((None, d, tn), lambda n, m: (layer, 0, n + off)),
        ],
        out_specs=pl.BlockSpec((tm, tn), lambda n, m: (m, n)),
        out_shape=jax.ShapeDtypeStruct((t, width), out_dtype),
        compiler_params=_cparams("parallel", "arbitrary"),
        name="in_proj_gates" if act else "in_proj_mix",
    )(xb, w)


def _band_attn_kernel(slope_ref, sink_ref, q_ref, k_ref, v_ref, kp_ref, vp_ref, *rest,
                      dil, nblk, max_dist, sb_per_seq, gqa, use_sinks, want_lse):
    if want_lse:
        o_ref, lse_ref, kbuf, vbuf, bias_ref = rest
    else:
        o_ref, kbuf, vbuf, bias_ref = rest
        lse_ref = None
    i = pl.program_id(0)
    p = pl.program_id(1)
    bw = BLOCK * dil
    first_in_seq = (i % sb_per_seq) == 0
    lane = lax.broadcasted_iota(jnp.int32, (1, LANES), 1)
    low = lane < HEAD_DIM

    def stage(cur_ref, prev_ref, buf):
        cur = cur_ref[...]
        prev = prev_ref[...]
        if gqa:
            kv_first = (p // 2) == 0

            def dup(a):
                r = pltpu.roll(a, HEAD_DIM, 1)
                return jnp.where(kv_first, jnp.where(low, a, r), jnp.where(low, r, a))

            cur, prev = dup(cur), dup(prev)
        buf[0:bw, :] = prev.astype(buf.dtype)
        buf[bw:, :] = cur.astype(buf.dtype)

    if gqa:
        @pl.when(p % 2 == 0)
        def _():
            stage(k_ref, kp_ref, kbuf)
            stage(v_ref, vp_ref, vbuf)
    else:
        stage(k_ref, kp_ref, kbuf)
        stage(v_ref, vp_ref, vbuf)

    q_off = lax.broadcasted_iota(jnp.int32, (2 * BLOCK, 2 * BLOCK), 0) % BLOCK
    k_off = lax.broadcasted_iota(jnp.int32, (2 * BLOCK, 2 * BLOCK), 1)
    dist = q_off + BLOCK - k_off
    in_band = (dist >= 0) & (dist <= max_dist)
    is_cur = k_off >= BLOCK
    dist_f = (jnp.maximum(dist, 0) * dil).astype(F32)
    top = lax.broadcasted_iota(jnp.int32, (2 * BLOCK, 1), 0) < BLOCK
    neg_slope = jnp.where(top, -slope_ref[2 * p], -slope_ref[2 * p + 1])
    bias = jnp.where(in_band, neg_slope * dist_f, NEG_BIG)
    bias_ref[0] = bias
    bias_ref[1] = jnp.where(is_cur, bias, NEG_BIG)
    if use_sinks:
        sink = jnp.where(top, sink_ref[2 * p], sink_ref[2 * p + 1])
    ones = jnp.ones((2 * BLOCK, LANES), BF16)

    def body(it, carry):
        j = it // dil
        c = it % dil
        row0 = j * bw + c
        if dil == 1:
            row0 = pl.multiple_of(row0, BLOCK)
            qs = pl.ds(row0, BLOCK)
            ks = pl.ds(row0, 2 * BLOCK)
        else:
            qs = pl.ds(row0, BLOCK, stride=dil)
            ks = pl.ds(row0, 2 * BLOCK, stride=dil)
        qc = q_ref[qs, :] * (HEAD_DIM ** -0.5)
        kk = kbuf[ks, :].astype(BF16)
        vv = vbuf[ks, :].astype(BF16)
        q2 = jnp.concatenate([jnp.where(low, qc, 0.0), jnp.where(low, 0.0, qc)], axis=0).astype(BF16)
        tbl = bias_ref[jnp.logical_and(j == 0, first_in_seq).astype(jnp.int32)]
        s = jnp.where(tbl > 0.5 * NEG_BIG, _dot_nt(q2, kk) + tbl, NEG_BIG)
        m = jnp.max(s, axis=-1, keepdims=True)
        if use_sinks:
            m = jnp.maximum(m, sink)
        prb = jnp.exp(s - m).astype(BF16)
        ov = _dot(prb, jnp.concatenate([vv, ones], axis=1))
        den = ov[:, LANES:]
        if use_sinks:
            den = den + jnp.exp(sink - m)
        on = ov[:, :LANES] / den
        o_ref[qs, :] = jnp.where(low, on[0:BLOCK], on[BLOCK:]).astype(o_ref.dtype)
        if want_lse:
            lse = m + jnp.log(den)
            lse_ref[qs, :] = jnp.where(low, lse[0:BLOCK], lse[BLOCK:])
        return carry

    lax.fori_loop(0, nblk * dil, body, 0, unroll=ATTN_UNROLL)


def _band_attn(z, slopes, sinks, *, seq, q_col, k_col, v_col, n_pairs, dil, max_dist, gqa, use_sinks,
               want_lse, out_dtype):
    t = z.shape[0]
    bw = BLOCK * dil
    sb = min(max(2 * bw, 2048), seq)
    nblk = sb // bw
    qb, kb, vb = q_col // LANES, k_col // LANES, v_col // LANES
    if gqa:
        kv_map = lambda i, p: (i, kb)
        vv_map = lambda i, p: (i, vb)
        kp_map = lambda i, p: (jnp.maximum(i * nblk - 1, 0), kb)
        vp_map = lambda i, p: (jnp.maximum(i * nblk - 1, 0), vb)
    else:
        kv_map = lambda i, p: (i, kb + p)
        vv_map = lambda i, p: (i, vb + p)
        kp_map = lambda i, p: (jnp.maximum(i * nblk - 1, 0), kb + p)
        vp_map = lambda i, p: (jnp.maximum(i * nblk - 1, 0), vb + p)
    smem = pl.BlockSpec(memory_space=pltpu.SMEM)
    out_spec = pl.BlockSpec((sb, LANES), lambda i, p: (i, p))
    out_specs = [out_spec]
    out_shape = [jax.ShapeDtypeStruct((t, n_pairs * LANES), out_dtype)]
    if want_lse:
        out_specs.append(out_spec)
        out_shape.append(jax.ShapeDtypeStruct((t, n_pairs * LANES), F32))
    return pl.pallas_call(
        functools.partial(_band_attn_kernel, dil=dil, nblk=nblk, max_dist=max_dist, sb_per_seq=seq // sb,
                          gqa=gqa, use_sinks=use_sinks, want_lse=want_lse),
        grid=(t // sb, n_pairs),
        in_specs=[
            smem,
            smem,
            pl.BlockSpec((sb, LANES), lambda i, p: (i, qb + p)),
            pl.BlockSpec((sb, LANES), kv_map),
            pl.BlockSpec((sb, LANES), vv_map),
            pl.BlockSpec((bw, LANES), kp_map),
            pl.BlockSpec((bw, LANES), vp_map),
        ],
        out_specs=out_specs,
        out_shape=out_shape,
        scratch_shapes=[pltpu.VMEM((bw + sb, LANES), BF16 if dil == 1 else F32)] * 2
        + [pltpu.VMEM((2, 2 * BLOCK, 2 * BLOCK), F32)],
        compiler_params=_cparams("parallel", "arbitrary"),
        name=f"band_attn_d{dil}" + ("_gqa" if gqa else ""),
    )(slopes, sinks, z, z, z, z, z)


def _gelu(x):
    return 0.5 * x * (1.0 + lax.erf(x * np.float32(np.sqrt(0.5))))


def _gmlp_kernel(u0_ref, u1_ref, v0_ref, v1_ref, g_ref, b_ref, ws_ref, bs_ref, o_ref):
    rows = u0_ref.shape[0]
    u = _gelu(jnp.concatenate([u0_ref[...], u1_ref[...]], axis=1))
    v = _gelu(jnp.concatenate([v0_ref[...], v1_ref[...]], axis=1))
    v = _layer_norm(v, g_ref[...], b_ref[...]).astype(BF16)
    t_idx = lax.broadcasted_iota(jnp.int32, (B_CHUNK, B_CHUNK), 0)
    s_idx = lax.broadcasted_iota(jnp.int32, (B_CHUNK, B_CHUNK), 1)
    causal = t_idx >= s_idx
    for grp in range(B_GROUPS):
        w = jnp.where(causal, ws_ref[grp], 0.0).astype(BF16)
        bias = bs_ref[:, grp:grp + 1]
        cols = slice(grp * B_CHUNK, (grp + 1) * B_CHUNK)
        for ch in range(rows // B_CHUNK):
            rs = slice(ch * B_CHUNK, (ch + 1) * B_CHUNK)
            mixed = _dot(w, v[rs, cols]) + bias
            o_ref[rs, cols] = (u[rs, cols] * mixed).astype(o_ref.dtype)


def _gmlp(z, ln_g, ln_b, w_s, b_s_t):
    t = z.shape[0]
    rows = min(2048, t)
    half = B_WIDTH // 2
    ub, vb = COL_UB // half, COL_VB // half
    col = lambda j: pl.BlockSpec((rows, half), lambda i: (i, j))
    vec = pl.BlockSpec((1, B_WIDTH), lambda i: (0, 0))
    return pl.pallas_call(
        _gmlp_kernel,
        grid=(t // rows,),
        in_specs=[col(ub), col(ub + 1), col(vb), col(vb + 1), vec, vec,
                  pl.BlockSpec((B_GROUPS, B_CHUNK, B_CHUNK), lambda i: (0, 0, 0)),
                  pl.BlockSpec((B_CHUNK, B_GROUPS), lambda i: (0, 0))],
        out_specs=pl.BlockSpec((rows, B_WIDTH), lambda i: (i, 0)),
        out_shape=jax.ShapeDtypeStruct((t, B_WIDTH), BF16),
        compiler_params=_cparams("parallel"),
        name="spatial_gating",
    )(z, z, z, z, ln_g, ln_b, w_s, b_s_t)


def _hgrn_tables():
    n = HGRN_CHUNK
    t = np.arange(n)[:, None]
    i = np.arange(n)[None, :]
    tabs = []
    for h in HGRN_LEVELS:
        start = (t // h) * h
        upper = ((t // h) % 2) == 1
        q_tab = (i >= start) & (i <= t)
        k_tab = (i > t) & (i < start + h)
        tabs.append(np.where(upper, q_tab, k_tab))
    tabs.append(i <= t)
    tabs.append(i > t)
    w = np.concatenate(tabs, axis=0).astype(np.float32)
    s = np.arange(n)[None, :]
    x = np.bitwise_xor(t, s)
    level = np.where(s < t, np.floor(np.log2(np.maximum(x, 1))).astype(np.int32), np.where(s == t, -1, -2))
    return w, level.astype(np.int32)


def _hgrn_kernel(lbl_ref, ng_ref, w_ref, lvl_ref, q0_ref, q1_ref, f0_ref, f1_ref, i0_ref, i1_ref, g0_ref, g1_ref,
                 o_ref, st_ref, *, layer):
    n = HGRN_CHUNK

    @pl.when(pl.program_id(1) == 0)
    def _():
        st_ref[...] = jnp.zeros_like(st_ref)

    logits = lbl_ref[...]
    e = jnp.exp(logits - jnp.max(logits, axis=0, keepdims=True))
    probs = e / jnp.sum(e, axis=0, keepdims=True)
    lb = jnp.zeros((1, C_WIDTH), F32)
    for j in range(1, layer + 1):
        lb = lb + probs[j:j + 1, :]

    w = w_ref[...]
    lvl = lvl_ref[...]
    row = lax.broadcasted_iota(jnp.int32, (n, C_WIDTH), 0)
    ng = ng_ref[...]
    nl = len(HGRN_LEVELS)
    for ch in range(q0_ref.shape[0] // n):
        rs = slice(ch * n, (ch + 1) * n)
        both = lambda a_ref, b_ref: jnp.concatenate([a_ref[rs, :], b_ref[rs, :]], axis=1)
        z = both(f0_ref, f1_ref)
        q = both(q0_ref, q1_ref)
        f = lb + (1.0 - lb) * jax.nn.sigmoid(z)
        logf = jnp.log(jnp.maximum(f, F_MIN))
        kk = (1.0 - lb) * jax.nn.sigmoid(-z)
        vb = both(i0_ref, i1_ref).astype(BF16)
        l1 = logf.astype(BF16)
        l2 = (logf - l1.astype(F32)).astype(BF16)
        expo = _dot(w, l1) + _dot(w, l2)
        xs = []
        for li, h in enumerate(HGRN_LEVELS):
            upper = (row & h) != 0
            xs.append((jnp.where(upper, q, kk) * jnp.exp(expo[li * n:(li + 1) * n])).astype(BF16))
        x1 = jnp.where((row & 1) != 0, q * jnp.maximum(f, F_MIN), kk).astype(BF16)
        q_in = (q * jnp.exp(expo[nl * n:(nl + 1) * n])).astype(BF16)
        k_out = (kk * jnp.exp(expo[(nl + 1) * n:(nl + 2) * n])).astype(BF16)
        carry = jnp.exp(expo[(nl + 1) * n - 1:(nl + 1) * n])
        qb, kb = q.astype(BF16), kk.astype(BF16)
        outs = []
        for hd in range(C_HEADS):
            cs = slice(hd * C_DIM, (hd + 1) * C_DIM)
            scores = jnp.where(lvl == -1, _dot_nt(qb[:, cs], kb[:, cs]), 0.0)
            scores = scores + jnp.where(lvl == 0, _dot_nt(x1[:, cs], x1[:, cs]), 0.0)
            for li, h in enumerate(HGRN_LEVELS):
                scores = scores + jnp.where(lvl == int(np.log2(h)), _dot_nt(xs[li][:, cs], xs[li][:, cs]), 0.0)
            st = st_ref[hd]
            o = _dot(scores.astype(BF16), vb[:, cs]) + _dot_nt(q_in[:, cs], st.astype(BF16))
            st_ref[hd] = st * carry[:, cs] + _dot_tn(vb[:, cs], k_out[:, cs])
            outs.append(o * lax.rsqrt(jnp.mean(o * o, axis=-1, keepdims=True) + LN_EPS))
        o = jnp.concatenate(outs, axis=1) * ng
        o_ref[rs, :] = (o * jax.nn.sigmoid(both(g0_ref, g1_ref))).astype(o_ref.dtype)


def _hgrn(z, lb_logits, norm_g, layer, batch, seq):
    t = z.shape[0]
    rows = min(1024, seq)
    per_seq = seq // rows
    half = C_WIDTH // 2
    w, lvl = _hgrn_tables()
    col = lambda c0, j: pl.BlockSpec((rows, half), lambda b, c: (b * per_seq + c, c0 // half + j))
    const = lambda shape: pl.BlockSpec(shape, lambda b, c: (0,) * len(shape))
    return pl.pallas_call(
        functools.partial(_hgrn_kernel, layer=layer),
        grid=(batch, per_seq),
        in_specs=[const(lb_logits.shape), const(norm_g.shape), const(w.shape), const(lvl.shape)]
        + [col(c0, j) for c0 in (COL_QC, COL_FC, COL_IC, COL_GC) for j in (0, 1)],
        out_specs=pl.BlockSpec((rows, C_WIDTH), lambda b, c: (b * per_seq + c, 0)),
        out_shape=jax.ShapeDtypeStruct((t, C_WIDTH), BF16),
        scratch_shapes=[pltpu.VMEM((C_HEADS, C_DIM, C_DIM), F32)],
        compiler_params=_cparams("parallel", "arbitrary"),
        name="hgrn2",
    )(lb_logits, norm_g, jnp.asarray(w, BF16), jnp.asarray(lvl), *([z] * 8))


def _merge_kernel(x_ref, oa_ref, ob_ref, oc_ref, d0_ref, d1_ref, d2_ref, l0_ref, l1_ref, l2_ref, gt_ref,
                  wa_ref, wb_ref, wc_ref, wd_ref, wo_ref, g_ref, b_ref, o_ref, obf_ref, y_ref, *, alpha):
    d = x_ref.shape[1]
    i = pl.program_id(0)
    nt = pl.num_programs(0) - 1

    def finish_previous():
        o = _layer_norm(y_ref[...], g_ref[...], b_ref[...])
        o_ref[...] = o
        obf_ref[...] = o.astype(BF16)

    @pl.when(i == 0)
    def _():
        y_ref[...] = jnp.zeros_like(y_ref)

    @pl.when(i < nt)
    def _():
        finish_previous()
        l0, l1, l2 = l0_ref[...], l1_ref[...], l2_ref[...]
        m = jnp.maximum(jnp.maximum(l0, l1), l2)
        e0, e1, e2 = jnp.exp(l0 - m), jnp.exp(l1 - m), jnp.exp(l2 - m)
        inv = 1.0 / (e0 + e1 + e2)
        od = (e0 * inv) * d0_ref[...] + (e1 * inv) * d1_ref[...] + (e2 * inv) * d2_ref[...]
        merged = gt_ref[:, 0:d].astype(F32) * _dot(oa_ref[...], wa_ref[...])
        merged = merged + gt_ref[:, d:2 * d].astype(F32) * _dot(ob_ref[...], wb_ref[...])
        merged = merged + gt_ref[:, 2 * d:3 * d].astype(F32) * _dot(oc_ref[...], wc_ref[...])
        merged = merged + gt_ref[:, 3 * d:4 * d].astype(F32) * _dot(od.astype(BF16), wd_ref[...])
        y_ref[...] = alpha * x_ref[...] + _dot(merged.astype(BF16), wo_ref[...])

    @pl.when(i == nt)
    def _():
        finish_previous()


def _merge(x, oa, ob, oc, ods, lses, gates, wa, wb, wc, wd, wo, layer, ln_g, ln_b, alpha):
    t, d = x.shape
    tm = min(256, t)
    nt = t // tm
    row = lambda width: pl.BlockSpec((tm, width), lambda i: (jnp.minimum(i, nt - 1), 0))
    lag = pl.BlockSpec((tm, d), lambda i: (jnp.maximum(i - 1, 0), 0))
    wspec = lambda k: pl.BlockSpec((None, k, d), lambda i: (layer, 0, 0), pipeline_mode=pl.Buffered(1))
    vec = pl.BlockSpec((1, d), lambda i: (0, 0))
    return pl.pallas_call(
        functools.partial(_merge_kernel, alpha=alpha),
        grid=(nt + 1,),
        in_specs=[row(d), row(A_WIDTH), row(B_WIDTH), row(C_WIDTH)] + [row(D_GROUP_WIDTH)] * 6
        + [row(N_BRANCHES * d), wspec(A_WIDTH), wspec(B_WIDTH), wspec(C_WIDTH), wspec(D_GROUP_WIDTH), wspec(d),
           vec, vec],
        out_specs=[lag, lag],
        out_shape=[jax.ShapeDtypeStruct((t, d), F32), jax.ShapeDtypeStruct((t, d), BF16)],
        scratch_shapes=[pltpu.VMEM((tm, d), F32)],
        compiler_params=_cparams("arbitrary"),
        name="merge_out_ln",
    )(x, oa, ob, oc, *ods, *lses, gates, wa, wb, wc, wd, wo, ln_g, ln_b)


def _ple_kernel(xb_ref, p_ref, wg_ref, wp_ref, o_ref):
    pb = p_ref[...].astype(BF16)
    for c0 in range(0, o_ref.shape[1], EPILOGUE_PANEL):
        cs = slice(c0, c0 + EPILOGUE_PANEL)
        o_ref[:, cs] = jax.nn.sigmoid(_dot(xb_ref[...], wg_ref[:, cs])) * _dot(pb, wp_ref[:, cs])


def _ple(xb, p, wg, wp, layer):
    t, d = xb.shape
    pd = p.shape[-1]
    tm = min(1024, t)
    tn = min(1024, d)
    return pl.pallas_call(
        _ple_kernel,
        grid=(d // tn, t // tm),
        in_specs=[
            pl.BlockSpec((tm, d), lambda n, m: (m, 0)),
            pl.BlockSpec((None, tm, pd), lambda n, m: (layer, m, 0)),
            pl.BlockSpec((None, d, tn), lambda n, m: (layer, 0, n)),
            pl.BlockSpec((None, pd, tn), lambda n, m: (layer, 0, n)),
        ],
        out_specs=pl.BlockSpec((tm, tn), lambda n, m: (m, n)),
        out_shape=jax.ShapeDtypeStruct((t, d), F32),
        compiler_params=_cparams("parallel", "arbitrary"),
        name="ple_gate",
    )(xb, p, wg, wp)


def _alibi_slopes(n):
    return jnp.exp2(-8.0 * jnp.arange(1, n + 1, dtype=F32) / n)


def kernel(x, p, ln_g, ln_b, ffn_w_gate, ffn_w_up, ffn_w_down, w_in, attn_sinks, gmlp_ln_g, gmlp_ln_b, gmlp_w_s,
           gmlp_b_s, hgrn_lb_logits, hgrn_norm_g, w_br_a, w_br_b, w_br_c, w_br_d, w_out, ple_w_proj, ple_w_gate):
    batch, seq, d = x.shape
    depth = ln_g.shape[0]
    t = batch * seq
    alpha = (2.0 * depth) ** 0.25
    assert seq % (BLOCK * D_PATTERNS[-1][1]) == 0, "the widest dilation needs whole 128-row residue blocks"
    assert w_in.shape[-1] == MIX_WIDTH + N_BRANCHES * d
    bf =lambda a: a.astype(BF16)
    wg, wu, wd = bf(ffn_w_gate), bf(ffn_w_up), bf(ffn_w_down)
    w_in_b = bf(w_in)
    wa, wb, wc, wdd, wo = bf(w_br_a), bf(w_br_b), bf(w_br_c), bf(w_br_d), bf(w_out)
    wpp, wpg = bf(ple_w_proj), bf(ple_w_gate)
    p2 = p.reshape(depth, t, p.shape[-1])
    slopes = _alibi_slopes(N_SOFTMAX_HEADS)
    no_sinks = jnp.zeros((D_HEADS,), F32)

    xf = x.reshape(t, d)
    for i in range(depth):
        xf, xb = _ffn(xf, wg, wu, wd, i, 0, ln_g[i, 0][None], ln_b[i, 0][None], None, alpha)
        z = _proj(xb, w_in_b, i, 0, MIX_WIDTH, F32, False)
        gates = _proj(xb, w_in_b, i, MIX_WIDTH, N_BRANCHES * d, BF16, True)
        oa = _band_attn(z, slopes[:A_Q_HEADS], attn_sinks[i], seq=seq, q_col=COL_QA, k_col=COL_KA, v_col=COL_VA,
                        n_pairs=A_Q_HEADS // 2, dil=1, max_dist=A_WINDOW - 1, gqa=True, use_sinks=True,
                        want_lse=False, out_dtype=BF16)[0]
        ob = _gmlp(z, gmlp_ln_g[i][None], gmlp_ln_b[i][None], gmlp_w_s[i], gmlp_b_s[i].T)
        oc = _hgrn(z, hgrn_lb_logits, hgrn_norm_g[i][None], i, batch, seq)
        ods, lses = [], []
        for g, (window, dil) in enumerate(D_PATTERNS):
            h0 = A_Q_HEADS + g * D_HEADS
            od, lse = _band_attn(z, slopes[h0:h0 + D_HEADS], no_sinks, seq=seq, q_col=COL_QD + g * D_GROUP_WIDTH,
                                 k_col=COL_KD + g * D_GROUP_WIDTH, v_col=COL_VD + g * D_GROUP_WIDTH,
                                 n_pairs=D_HEADS // 2, dil=dil, max_dist=window // dil, gqa=False, use_sinks=False,
                                 want_lse=True, out_dtype=F32)
            ods.append(od)
            lses.append(lse)
        xf, xb = _merge(xf, oa, ob, oc, ods, lses, gates, wa, wb, wc, wdd, wo, i, ln_g[i, 1][None], ln_b[i, 1][None],
                        alpha)
        ple = _ple(xb, p2, wpg, wpp, i)
        xf, xb = _ffn(xf, wg, wu, wd, i, 1, ln_g[i, 2][None], ln_b[i, 2][None], ple, alpha)
    return xf.reshape(batch, seq, d)
```

```python
import functools

import numpy as np

import jax
import jax.numpy as jnp
from jax import lax
from jax.experimental import pallas as pl
from jax.experimental.pallas import tpu as pltpu

F32 = jnp.float32
BF16 = jnp.bfloat16

LANES = 128
VMEM_LIMIT_BYTES = 60 * 1024 * 1024

HEAD_DIM = 64
BLOCK = 128
LN_EPS = 1e-5
NEG_BIG = -1e30
F_MIN = 1e-6
A_Q_HEADS = 8
A_KV_HEADS = 2
A_WINDOW = 128
B_GROUPS = 4
B_CHUNK = 128
C_HEADS = 4
C_DIM = 128
D_PATTERNS = ((128, 1), (512, 4), (2048, 16))
D_HEADS = 4
N_SOFTMAX_HEADS = A_Q_HEADS + len(D_PATTERNS) * D_HEADS
N_BRANCHES = 4
A_WIDTH = A_Q_HEADS * HEAD_DIM
A_KV_WIDTH = A_KV_HEADS * HEAD_DIM
B_WIDTH = B_GROUPS * B_CHUNK
C_WIDTH = C_HEADS * C_DIM
D_GROUP_WIDTH = D_HEADS * HEAD_DIM
D_WIDTH = len(D_PATTERNS) * D_GROUP_WIDTH
COL_QA = 0
COL_KA = COL_QA + A_WIDTH
COL_VA = COL_KA + A_KV_WIDTH
COL_UB = COL_VA + A_KV_WIDTH
COL_VB = COL_UB + B_WIDTH
COL_QC = COL_VB + B_WIDTH
COL_FC = COL_QC + C_WIDTH
COL_IC = COL_FC + C_WIDTH
COL_GC = COL_IC + C_WIDTH
COL_QD = COL_GC + C_WIDTH
COL_KD = COL_QD + D_WIDTH
COL_VD = COL_KD + D_WIDTH
MIX_WIDTH = COL_VD + D_WIDTH

EPILOGUE_PANEL = 256
FFN_RING_SLOTS = 3
ATTN_UNROLL = 8
HGRN_CHUNK = 128
HGRN_LEVELS = (64, 32, 16, 8, 4, 2)


def _cparams(*sem):
    return pltpu.CompilerParams(dimension_semantics=sem, vmem_limit_bytes=VMEM_LIMIT_BYTES)


def _layer_norm(v, g, b):
    mu = jnp.mean(v, axis=-1, keepdims=True)
    d = v - mu
    var = jnp.mean(d * d, axis=-1, keepdims=True)
    return d * lax.rsqrt(var + LN_EPS) * g + b


def _dot(a, b):
    return jnp.dot(a, b, preferred_element_type=F32)


def _dot_nt(a, b):
    return lax.dot_general(a, b, (((1,), (1,)), ((), ())), preferred_element_type=F32)


def _dot_tn(a, b):
    return lax.dot_general(a, b, (((0,), (0,)), ((), ())), preferred_element_type=F32)


def _ffn_kernel(x_ref, wg_hbm, wu_hbm, wd_hbm, g_ref, b_ref, *rest, alpha, has_extra, layer, which, nt, nf, tf):
    ahead = FFN_RING_SLOTS - 1
    if has_extra:
        e_ref, o_ref, ob_ref, xb_ref, acc_ref, wgu_buf, wd_buf, sem = rest
    else:
        o_ref, ob_ref, xb_ref, acc_ref, wgu_buf, wd_buf, sem = rest
    i = pl.program_id(0)
    base = i * nf

    def copies(pos):
        slot = pos % FFN_RING_SLOTS
        off = (pos % nf) * tf
        if not isinstance(off, int):
            off = pl.multiple_of(off, tf)
        gate_dst = wgu_buf.at[slot, :, pl.ds(0, tf)]
        up_dst = wgu_buf.at[slot, :, pl.ds(tf, tf)]
        return (
            pltpu.make_async_copy(wg_hbm.at[layer, which, :, pl.ds(off, tf)], gate_dst, sem.at[0, slot]),
            pltpu.make_async_copy(wu_hbm.at[layer, which, :, pl.ds(off, tf)], up_dst, sem.at[1, slot]),
            pltpu.make_async_copy(wd_hbm.at[layer, which, pl.ds(off, tf), :], wd_buf.at[slot], sem.at[2, slot]),
        )

    def fetch(pos):
        for c in copies(pos):
            c.wait()

        @pl.when(pos + ahead < nt * nf)
        def _():
            for c in copies(pos + ahead):
                c.start()

    def contribution(pos):
        slot = pos % FFN_RING_SLOTS
        gu = _dot(xb_ref[...], wgu_buf[slot])
        g, u = gu[:, :tf], gu[:, tf:]
        h = (g * jax.nn.sigmoid(g) * u).astype(BF16)
        return _dot(h, wd_buf[slot])

    def finish_previous():
        y = 0.5 * acc_ref[...]
        if has_extra:
            y = y + e_ref[...]
        o = _layer_norm(y, g_ref[...], b_ref[...])
        o_ref[...] = o
        ob_ref[...] = o.astype(BF16)

    @pl.when(i == 0)
    def _():
        acc_ref[...] = jnp.zeros_like(acc_ref)
        for pos in range(min(ahead, nt * nf)):
            for c in copies(pos):
                c.start()

    @pl.when(i < nt)
    def _():
        fetch(base)
        finish_previous()
        xb_ref[...] = x_ref[...].astype(BF16)
        acc_ref[...] = (2.0 * alpha) * x_ref[...] + contribution(base)

        def chunk(f, carry):
            fetch(base + f)
            acc_ref[...] = acc_ref[...] + contribution(base + f)
            return carry

        lax.fori_loop(1, nf, chunk, 0)

    @pl.when(i == nt)
    def _():
        finish_previous()


def _ffn(x, wg, wu, wd, layer, which, ln_g, ln_b, extra, alpha, tm=512, tf=512):
    t, d = x.shape
    ff = wg.shape[-1]
    tm = min(tm, t)
    tf = min(tf, ff)
    nt, nf = t // tm, ff // tf
    cur = pl.BlockSpec((tm, d), lambda i: (jnp.minimum(i, nt - 1), 0))
    lag = pl.BlockSpec((tm, d), lambda i: (jnp.maximum(i - 1, 0), 0))
    vec = pl.BlockSpec((1, d), lambda i: (0, 0))
    hbm = pl.BlockSpec(memory_space=pl.ANY)
    in_specs = [cur, hbm, hbm, hbm, vec, vec]
    args = [x, wg, wu, wd, ln_g, ln_b]
    if extra is not None:
        in_specs.append(lag)
        args.append(extra)
    return pl.pallas_call(
        functools.partial(_ffn_kernel, alpha=alpha, has_extra=extra is not None, layer=layer, which=which, nt=nt,
                          nf=nf, tf=tf),
        grid=(nt + 1,),
        in_specs=in_specs,
        out_specs=[lag, lag],
        out_shape=[jax.ShapeDtypeStruct((t, d), F32), jax.ShapeDtypeStruct((t, d), BF16)],
        scratch_shapes=[pltpu.VMEM((tm, d), BF16), pltpu.VMEM((tm, d), F32),
                        pltpu.VMEM((FFN_RING_SLOTS, d, 2 * tf), BF16), pltpu.VMEM((FFN_RING_SLOTS, tf, d), BF16),
                        pltpu.SemaphoreType.DMA((3, FFN_RING_SLOTS))],
        compiler_params=_cparams("arbitrary"),
        name="ffn_ln",
    )(*args)


def _proj_kernel(x_ref, w_ref, o_ref, *, act):
    z = _dot(x_ref[...], w_ref[...])
    if act:
        z = jax.nn.sigmoid(z)
    o_ref[...] = z.astype(o_ref.dtype)


def _proj(xb, w, layer, col0, width, out_dtype, act):
    t, d = xb.shape
    tm = min(2048, t)
    tn = min(1024, width)
    off = col0 // tn
    return pl.pallas_call(
        functools.partial(_proj_kernel, act=act),
        grid=(width // tn, t // tm),
        in_specs=[
            pl.BlockSpec((tm, d), lambda n, m: (m, 0)),
            pl.BlockSpec((None, d, tn), lambda n, m: (layer, 0, n + off)),
        ],
        out_specs=pl.BlockSpec((tm, tn), lambda n, m: (m, n)),
        out_shape=jax.ShapeDtypeStruct((t, width), out_dtype),
        compiler_params=_cparams("parallel", "arbitrary"),
        name="in_proj_gates" if act else "in_proj_mix",
    )(xb, w)


def _band_attn_kernel(slope_ref, sink_ref, q_ref, k_ref, v_ref, kp_ref, vp_ref, *rest,
                      dil, nblk, max_dist, sb_per_seq, gqa, use_sinks, want_lse):
    if want_lse:
        o_ref, lse_ref, kbuf, vbuf, bias_ref = rest
    else:
        o_ref, kbuf, vbuf, bias_ref = rest
        lse_ref = None
    i = pl.program_id(0)
    p = pl.program_id(1)
    bw = BLOCK * dil
    first_in_seq = (i % sb_per_seq) == 0
    lane = lax.broadcasted_iota(jnp.int32, (1, LANES), 1)
    low = lane < HEAD_DIM

    def stage(cur_ref, prev_ref, buf):
        cur = cur_ref[...]
        prev = prev_ref[...]
        if gqa:
            kv_first = (p // 2) == 0

            def dup(a):
                r = pltpu.roll(a, HEAD_DIM, 1)
                return jnp.where(kv_first, jnp.where(low, a, r), jnp.where(low, r, a))

            cur, prev = dup(cur), dup(prev)
        buf[0:bw, :] = prev.astype(buf.dtype)
        buf[bw:, :] = cur.astype(buf.dtype)

    if gqa:
        @pl.when(p % 2 == 0)
        def _():
            stage(k_ref, kp_ref, kbuf)
            stage(v_ref, vp_ref, vbuf)
    else:
        stage(k_ref, kp_ref, kbuf)
        stage(v_ref, vp_ref, vbuf)

    q_off = lax.broadcasted_iota(jnp.int32, (2 * BLOCK, 2 * BLOCK), 0) % BLOCK
    k_off = lax.broadcasted_iota(jnp.int32, (2 * BLOCK, 2 * BLOCK), 1)
    dist = q_off + BLOCK - k_off
    in_band = (dist >= 0) & (dist <= max_dist)
    is_cur = k_off >= BLOCK
    dist_f = (jnp.maximum(dist, 0) * dil).astype(F32)
    top = lax.broadcasted_iota(jnp.int32, (2 * BLOCK, 1), 0) < BLOCK
    neg_slope = jnp.where(top, -slope_ref[2 * p], -slope_ref[2 * p + 1])
    bias = jnp.where(in_band, neg_slope * dist_f, NEG_BIG)
    bias_ref[0] = bias
    bias_ref[1] = jnp.where(is_cur, bias, NEG_BIG)
    if use_sinks:
        sink = jnp.where(top, sink_ref[2 * p], sink_ref[2 * p + 1])
    ones = jnp.ones((2 * BLOCK, LANES), BF16)

    def body(it, carry):
        j = it // dil
        c = it % dil
        row0 = j * bw + c
        if dil == 1:
            row0 = pl.multiple_of(row0, BLOCK)
            qs = pl.ds(row0, BLOCK)
            ks = pl.ds(row0, 2 * BLOCK)
        else:
            qs = pl.ds(row0, BLOCK, stride=dil)
            ks = pl.ds(row0, 2 * BLOCK, stride=dil)
        qc = q_ref[qs, :] * (HEAD_DIM ** -0.5)
        kk = kbuf[ks, :].astype(BF16)
        vv = vbuf[ks, :].astype(BF16)
        q2 = jnp.concatenate([jnp.where(low, qc, 0.0), jnp.where(low, 0.0, qc)], axis=0).astype(BF16)
        tbl = bias_ref[jnp.logical_and(j == 0, first_in_seq).astype(jnp.int32)]
        s = jnp.where(tbl > 0.5 * NEG_BIG, _dot_nt(q2, kk) + tbl, NEG_BIG)
        m = jnp.max(s, axis=-1, keepdims=True)
        if use_sinks:
            m = jnp.maximum(m, sink)
        prb = jnp.exp(s - m).astype(BF16)
        ov = _dot(prb, jnp.concatenate([vv, ones], axis=1))
        pick = lambda a: jnp.where(low, a[0:BLOCK], a[BLOCK:])
        den = pick(ov[:, LANES:])
        if use_sinks:
            den = den + pick(jnp.exp(sink - m))
        o_ref[qs, :] = (pick(ov[:, :LANES]) / den).astype(o_ref.dtype)
        if want_lse:
            lse_ref[qs, :] = pick(m) + jnp.log(den)
        return carry

    lax.fori_loop(0, nblk * dil, body, 0, unroll=ATTN_UNROLL)


def _band_attn(z, slopes, sinks, *, seq, q_col, k_col, v_col, n_pairs, dil, max_dist, gqa, use_sinks,
               want_lse, out_dtype):
    t = z.shape[0]
    bw = BLOCK * dil
    sb = min(max(2 * bw, 2048), seq)
    nblk = sb // bw
    qb, kb, vb = q_col // LANES, k_col // LANES, v_col // LANES
    if gqa:
        kv_map = lambda i, p: (i, kb)
        vv_map = lambda i, p: (i, vb)
        kp_map = lambda i, p: (jnp.maximum(i * nblk - 1, 0), kb)
        vp_map = lambda i, p: (jnp.maximum(i * nblk - 1, 0), vb)
    else:
        kv_map = lambda i, p: (i, kb + p)
        vv_map = lambda i, p: (i, vb + p)
        kp_map = lambda i, p: (jnp.maximum(i * nblk - 1, 0), kb + p)
        vp_map = lambda i, p: (jnp.maximum(i * nblk - 1, 0), vb + p)
    smem = pl.BlockSpec(memory_space=pltpu.SMEM)
    out_spec = pl.BlockSpec((sb, LANES), lambda i, p: (i, p))
    out_specs = [out_spec]
    out_shape = [jax.ShapeDtypeStruct((t, n_pairs * LANES), out_dtype)]
    if want_lse:
        out_specs.append(out_spec)
        out_shape.append(jax.ShapeDtypeStruct((t, n_pairs * LANES), F32))
    return pl.pallas_call(
        functools.partial(_band_attn_kernel, dil=dil, nblk=nblk, max_dist=max_dist, sb_per_seq=seq // sb,
                          gqa=gqa, use_sinks=use_sinks, want_lse=want_lse),
        grid=(t // sb, n_pairs),
        in_specs=[
            smem,
            smem,
            pl.BlockSpec((sb, LANES), lambda i, p: (i, qb + p)),
            pl.BlockSpec((sb, LANES), kv_map),
            pl.BlockSpec((sb, LANES), vv_map),
            pl.BlockSpec((bw, LANES), kp_map),
            pl.BlockSpec((bw, LANES), vp_map),
        ],
        out_specs=out_specs,
        out_shape=out_shape,
        scratch_shapes=[pltpu.VMEM((bw + sb, LANES), BF16 if dil == 1 else F32)] * 2
        + [pltpu.VMEM((2, 2 * BLOCK, 2 * BLOCK), F32)],
        compiler_params=_cparams("parallel", "arbitrary"),
        name=f"band_attn_d{dil}" + ("_gqa" if gqa else ""),
    )(slopes, sinks, z, z, z, z, z)


def _gelu(x):
    return 0.5 * x * (1.0 + lax.erf(x * np.float32(np.sqrt(0.5))))


def _gmlp_kernel(u0_ref, u1_ref, v0_ref, v1_ref, g_ref, b_ref, ws_ref, bs_ref, o_ref):
    rows = u0_ref.shape[0]
    u = _gelu(jnp.concatenate([u0_ref[...], u1_ref[...]], axis=1))
    v = _gelu(jnp.concatenate([v0_ref[...], v1_ref[...]], axis=1))
    v = _layer_norm(v, g_ref[...], b_ref[...]).astype(BF16)
    t_idx = lax.broadcasted_iota(jnp.int32, (B_CHUNK, B_CHUNK), 0)
    s_idx = lax.broadcasted_iota(jnp.int32, (B_CHUNK, B_CHUNK), 1)
    causal = t_idx >= s_idx
    for grp in range(B_GROUPS):
        w = jnp.where(causal, ws_ref[grp], 0.0).astype(BF16)
        bias = bs_ref[:, grp:grp + 1]
        cols = slice(grp * B_CHUNK, (grp + 1) * B_CHUNK)
        for ch in range(rows // B_CHUNK):
            rs = slice(ch * B_CHUNK, (ch + 1) * B_CHUNK)
            mixed = _dot(w, v[rs, cols]) + bias
            o_ref[rs, cols] = (u[rs, cols] * mixed).astype(o_ref.dtype)


def _gmlp(z, ln_g, ln_b, w_s, b_s_t):
    t = z.shape[0]
    rows = min(2048, t)
    half = B_WIDTH // 2
    ub, vb = COL_UB // half, COL_VB // half
    col = lambda j: pl.BlockSpec((rows, half), lambda i: (i, j))
    vec = pl.BlockSpec((1, B_WIDTH), lambda i: (0, 0))
    return pl.pallas_call(
        _gmlp_kernel,
        grid=(t // rows,),
        in_specs=[col(ub), col(ub + 1), col(vb), col(vb + 1), vec, vec,
                  pl.BlockSpec((B_GROUPS, B_CHUNK, B_CHUNK), lambda i: (0, 0, 0)),
                  pl.BlockSpec((B_CHUNK, B_GROUPS), lambda i: (0, 0))],
        out_specs=pl.BlockSpec((rows, B_WIDTH), lambda i: (i, 0)),
        out_shape=jax.ShapeDtypeStruct((t, B_WIDTH), BF16),
        compiler_params=_cparams("parallel"),
        name="spatial_gating",
    )(z, z, z, z, ln_g, ln_b, w_s, b_s_t)


def _hgrn_tables():
    n = HGRN_CHUNK
    t = np.arange(n)[:, None]
    i = np.arange(n)[None, :]
    tabs = []
    for h in HGRN_LEVELS:
        start = (t // h) * h
        upper = ((t // h) % 2) == 1
        q_tab = (i >= start) & (i <= t)
        k_tab = (i > t) & (i < start + h)
        tabs.append(np.where(upper, q_tab, k_tab))
    tabs.append(i <= t)
    tabs.append(i > t)
    w = np.concatenate(tabs, axis=0).astype(np.float32)
    w = np.concatenate([w, w], axis=1)
    s = np.arange(n)[None, :]
    x = np.bitwise_xor(t, s)
    level = np.where(s < t, np.floor(np.log2(np.maximum(x, 1))).astype(np.int32), np.where(s == t, -1, -2))
    return w, level.astype(np.int32)


def _hgrn_kernel(lbl_ref, ng_ref, w_ref, lvl_ref, q0_ref, q1_ref, f0_ref, f1_ref, i0_ref, i1_ref, g0_ref, g1_ref,
                 o_ref, st_ref, *, layer):
    n = HGRN_CHUNK

    @pl.when(pl.program_id(1) == 0)
    def _():
        st_ref[...] = jnp.zeros_like(st_ref)

    logits = lbl_ref[...]
    e = jnp.exp(logits - jnp.max(logits, axis=0, keepdims=True))
    probs = e / jnp.sum(e, axis=0, keepdims=True)
    lb = jnp.zeros((1, C_WIDTH), F32)
    for j in range(1, layer + 1):
        lb = lb + probs[j:j + 1, :]

    w = w_ref[...]
    lvl = lvl_ref[...]
    row = lax.broadcasted_iota(jnp.int32, (n, C_WIDTH), 0)
    ng = ng_ref[...]
    nl = len(HGRN_LEVELS)
    for ch in range(q0_ref.shape[0] // n):
        rs = slice(ch * n, (ch + 1) * n)
        both = lambda a_ref, b_ref: jnp.concatenate([a_ref[rs, :], b_ref[rs, :]], axis=1)
        z = both(f0_ref, f1_ref)
        q = both(q0_ref, q1_ref)
        f = lb + (1.0 - lb) * jax.nn.sigmoid(z)
        logf = jnp.log(jnp.maximum(f, F_MIN))
        kk = (1.0 - lb) * jax.nn.sigmoid(-z)
        vb = both(i0_ref, i1_ref).astype(BF16)
        l1 = logf.astype(BF16)
        l2 = (logf - l1.astype(F32)).astype(BF16)
        expo = _dot(w, jnp.concatenate([l1, l2], axis=0))
        xs = []
        for li, h in enumerate(HGRN_LEVELS):
            upper = (row & h) != 0
            xs.append((jnp.where(upper, q, kk) * jnp.exp(expo[li * n:(li + 1) * n])).astype(BF16))
        x1 = jnp.where((row & 1) != 0, q * jnp.maximum(f, F_MIN), kk).astype(BF16)
        q_in = (q * jnp.exp(expo[nl * n:(nl + 1) * n])).astype(BF16)
        k_out = (kk * jnp.exp(expo[(nl + 1) * n:(nl + 2) * n])).astype(BF16)
        carry = jnp.exp(expo[(nl + 1) * n - 1:(nl + 1) * n])
        qb, kb = q.astype(BF16), kk.astype(BF16)
        outs = []
        for hd in range(C_HEADS):
            cs = slice(hd * C_DIM, (hd + 1) * C_DIM)
            scores = jnp.where(lvl == -1, _dot_nt(qb[:, cs], kb[:, cs]), 0.0)
            scores = scores + jnp.where(lvl == 0, _dot_nt(x1[:, cs], x1[:, cs]), 0.0)
            for li, h in enumerate(HGRN_LEVELS):
                scores = scores + jnp.where(lvl == int(np.log2(h)), _dot_nt(xs[li][:, cs], xs[li][:, cs]), 0.0)
            st = st_ref[hd]
            o = _dot(scores.astype(BF16), vb[:, cs]) + _dot_nt(q_in[:, cs], st.astype(BF16))
            st_ref[hd] = st * carry[:, cs] + _dot_tn(vb[:, cs], k_out[:, cs])
            outs.append(o * lax.rsqrt(jnp.mean(o * o, axis=-1, keepdims=True) + LN_EPS))
        o = jnp.concatenate(outs, axis=1) * ng
        o_ref[rs, :] = (o * jax.nn.sigmoid(both(g0_ref, g1_ref))).astype(o_ref.dtype)


def _hgrn(z, lb_logits, norm_g, layer, batch, seq):
    t = z.shape[0]
    rows = min(1024, seq)
    per_seq = seq // rows
    half = C_WIDTH // 2
    w, lvl = _hgrn_tables()
    col = lambda c0, j: pl.BlockSpec((rows, half), lambda b, c: (b * per_seq + c, c0 // half + j))
    const = lambda shape: pl.BlockSpec(shape, lambda b, c: (0,) * len(shape))
    return pl.pallas_call(
        functools.partial(_hgrn_kernel, layer=layer),
        grid=(batch, per_seq),
        in_specs=[const(lb_logits.shape), const(norm_g.shape), const(w.shape), const(lvl.shape)]
        + [col(c0, j) for c0 in (COL_QC, COL_FC, COL_IC, COL_GC) for j in (0, 1)],
        out_specs=pl.BlockSpec((rows, C_WIDTH), lambda b, c: (b * per_seq + c, 0)),
        out_shape=jax.ShapeDtypeStruct((t, C_WIDTH), BF16),
        scratch_shapes=[pltpu.VMEM((C_HEADS, C_DIM, C_DIM), F32)],
        compiler_params=_cparams("parallel", "arbitrary"),
        name="hgrn2",
    )(lb_logits, norm_g, jnp.asarray(w, BF16), jnp.asarray(lvl), *([z] * 8))


def _merge_kernel(x_ref, oa_ref, ob_ref, oc_ref, d0_ref, d1_ref, d2_ref, l0_ref, l1_ref, l2_ref, gt_ref,
                  wa_ref, wb_ref, wc_ref, wd_ref, wo_ref, g_ref, b_ref, o_ref, obf_ref, y_ref, *, alpha):
    d = x_ref.shape[1]
    i = pl.program_id(0)
    nt = pl.num_programs(0) - 1

    def finish_previous():
        o = _layer_norm(y_ref[...], g_ref[...], b_ref[...])
        o_ref[...] = o
        obf_ref[...] = o.astype(BF16)

    @pl.when(i == 0)
    def _():
        y_ref[...] = jnp.zeros_like(y_ref)

    @pl.when(i < nt)
    def _():
        finish_previous()
        l0, l1, l2 = l0_ref[...], l1_ref[...], l2_ref[...]
        m = jnp.maximum(jnp.maximum(l0, l1), l2)
        e0, e1, e2 = jnp.exp(l0 - m), jnp.exp(l1 - m), jnp.exp(l2 - m)
        inv = 1.0 / (e0 + e1 + e2)
        od = (e0 * inv) * d0_ref[...] + (e1 * inv) * d1_ref[...] + (e2 * inv) * d2_ref[...]
        merged = gt_ref[:, 0:d].astype(F32) * _dot(oa_ref[...], wa_ref[...])
        merged = merged + gt_ref[:, d:2 * d].astype(F32) * _dot(ob_ref[...], wb_ref[...])
        merged = merged + gt_ref[:, 2 * d:3 * d].astype(F32) * _dot(oc_ref[...], wc_ref[...])
        merged = merged + gt_ref[:, 3 * d:4 * d].astype(F32) * _dot(od.astype(BF16), wd_ref[...])
        y_ref[...] = alpha * x_ref[...] + _dot(merged.astype(BF16), wo_ref[...])

    @pl.when(i == nt)
    def _():
        finish_previous()


def _merge(x, oa, ob, oc, ods, lses, gates, wa, wb, wc, wd, wo, layer, ln_g, ln_b, alpha):
    t, d = x.shape
    tm = min(256, t)
    nt = t // tm
    row = lambda width: pl.BlockSpec((tm, width), lambda i: (jnp.minimum(i, nt - 1), 0))
    lag = pl.BlockSpec((tm, d), lambda i: (jnp.maximum(i - 1, 0), 0))
    wspec = lambda k: pl.BlockSpec((None, k, d), lambda i: (layer, 0, 0), pipeline_mode=pl.Buffered(1))
    vec = pl.BlockSpec((1, d), lambda i: (0, 0))
    return pl.pallas_call(
        functools.partial(_merge_kernel, alpha=alpha),
        grid=(nt + 1,),
        in_specs=[row(d), row(A_WIDTH), row(B_WIDTH), row(C_WIDTH)] + [row(D_GROUP_WIDTH)] * 6
        + [row(N_BRANCHES * d), wspec(A_WIDTH), wspec(B_WIDTH), wspec(C_WIDTH), wspec(D_GROUP_WIDTH), wspec(d),
           vec, vec],
        out_specs=[lag, lag],
        out_shape=[jax.ShapeDtypeStruct((t, d), F32), jax.ShapeDtypeStruct((t, d), BF16)],
        scratch_shapes=[pltpu.VMEM((tm, d), F32)],
        compiler_params=_cparams("arbitrary"),
        name="merge_out_ln",
    )(x, oa, ob, oc, *ods, *lses, gates, wa, wb, wc, wd, wo, ln_g, ln_b)


def _ple_kernel(xb_ref, p_ref, wg_ref, wp_ref, o_ref):
    pb = p_ref[...].astype(BF16)
    for c0 in range(0, o_ref.shape[1], EPILOGUE_PANEL):
        cs = slice(c0, c0 + EPILOGUE_PANEL)
        o_ref[:, cs] = jax.nn.sigmoid(_dot(xb_ref[...], wg_ref[:, cs])) * _dot(pb, wp_ref[:, cs])


def _ple(xb, p, wg, wp, layer):
    t, d = xb.shape
    pd = p.shape[-1]
    tm = min(1024, t)
    tn = min(1024, d)
    return pl.pallas_call(
        _ple_kernel,
        grid=(d // tn, t // tm),
        in_specs=[
            pl.BlockSpec((tm, d), lambda n, m: (m, 0)),
            pl.BlockSpec((None, tm, pd), lambda n, m: (layer, m, 0)),
            pl.BlockSpec((None, d, tn), lambda n, m: (layer, 0, n)),
            pl.BlockSpec((None, pd, tn), lambda n, m: (layer, 0, n)),
        ],
        out_specs=pl.BlockSpec((tm, tn), lambda n, m: (m, n)),
        out_shape=jax.ShapeDtypeStruct((t, d), F32),
        compiler_params=_cparams("parallel", "arbitrary"),
        name="ple_gate",
    )(xb, p, wg, wp)


def _alibi_slopes(n):
    return jnp.exp2(-8.0 * jnp.arange(1, n + 1, dtype=F32) / n)


def kernel(x, p, ln_g, ln_b, ffn_w_gate, ffn_w_up, ffn_w_down, w_in, attn_sinks, gmlp_ln_g, gmlp_ln_b, gmlp_w_s,
           gmlp_b_s, hgrn_lb_logits, hgrn_norm_g, w_br_a, w_br_b, w_br_c, w_br_d, w_out, ple_w_proj, ple_w_gate):
    batch, seq, d = x.shape
    depth = ln_g.shape[0]
    t = batch * seq
    alpha = (2.0 * depth) ** 0.25
    assert seq % (BLOCK * D_PATTERNS[-1][1]) == 0, "the widest dilation needs whole 128-row residue blocks"
    assert w_in.shape[-1] == MIX_WIDTH + N_BRANCHES * d
    bf =lambda a: a.astype(BF16)
    wg, wu, wd = bf(ffn_w_gate), bf(ffn_w_up), bf(ffn_w_down)
    w_in_b = bf(w_in)
    wa, wb, wc, wdd, wo = bf(w_br_a), bf(w_br_b), bf(w_br_c), bf(w_br_d), bf(w_out)
    wpp, wpg = bf(ple_w_proj), bf(ple_w_gate)
    p2 = p.reshape(depth, t, p.shape[-1])
    slopes = _alibi_slopes(N_SOFTMAX_HEADS)
    no_sinks = jnp.zeros((D_HEADS,), F32)

    xf = x.reshape(t, d)
    for i in range(depth):
        xf, xb = _ffn(xf, wg, wu, wd, i, 0, ln_g[i, 0][None], ln_b[i, 0][None], None, alpha)
        z = _proj(xb, w_in_b, i, 0, MIX_WIDTH, F32, False)
        gates = _proj(xb, w_in_b, i, MIX_WIDTH, N_BRANCHES * d, BF16, True)
        oa = _band_attn(z, slopes[:A_Q_HEADS], attn_sinks[i], seq=seq, q_col=COL_QA, k_col=COL_KA, v_col=COL_VA,
                        n_pairs=A_Q_HEADS // 2, dil=1, max_dist=A_WINDOW - 1, gqa=True, use_sinks=True,
                        want_lse=False, out_dtype=BF16)[0]
        ob = _gmlp(z, gmlp_ln_g[i][None], gmlp_ln_b[i][None], gmlp_w_s[i], gmlp_b_s[i].T)
        oc = _hgrn(z, hgrn_lb_logits, hgrn_norm_g[i][None], i, batch, seq)
        ods, lses = [], []
        for g, (window, dil) in enumerate(D_PATTERNS):
            h0 = A_Q_HEADS + g * D_HEADS
            od, lse = _band_attn(z, slopes[h0:h0 + D_HEADS], no_sinks, seq=seq, q_col=COL_QD + g * D_GROUP_WIDTH,
                                 k_col=COL_KD + g * D_GROUP_WIDTH, v_col=COL_VD + g * D_GROUP_WIDTH,
                                 n_pairs=D_HEADS // 2, dil=dil, max_dist=window // dil, gqa=False, use_sinks=False,
                                 want_lse=True, out_dtype=F32)
            ods.append(od)
            lses.append(lse)
        xf, xb = _merge(xf, oa, ob, oc, ods, lses, gates, wa, wb, wc, wdd, wo, i, ln_g[i, 1][None], ln_b[i, 1][None],
                        alpha)
        ple = _ple(xb, p2, wpg, wpp, i)
        xf, xb = _ffn(xf, wg, wu, wd, i, 1, ln_g[i, 2][None], ln_b[i, 2][None], ple, alpha)
    return xf.reshape(batch, seq, d)
```

```python
import functools

import numpy as np

import jax
import jax.numpy as jnp
from jax import lax
from jax.experimental import pallas as pl
from jax.experimental.pallas import tpu as pltpu

F32 = jnp.float32
BF16 = jnp.bfloat16

LANES = 128
VMEM_LIMIT_BYTES = 60 * 1024 * 1024

HEAD_DIM = 64
BLOCK = 128
LN_EPS = 1e-5
NEG_BIG = -1e30
F_MIN = 1e-6
A_Q_HEADS = 8
A_KV_HEADS = 2
A_WINDOW = 128
B_GROUPS = 4
B_CHUNK = 128
C_HEADS = 4
C_DIM = 128
D_PATTERNS = ((128, 1), (512, 4), (2048, 16))
D_HEADS = 4
N_SOFTMAX_HEADS = A_Q_HEADS + len(D_PATTERNS) * D_HEADS
N_BRANCHES = 4
A_WIDTH = A_Q_HEADS * HEAD_DIM
A_KV_WIDTH = A_KV_HEADS * HEAD_DIM
B_WIDTH = B_GROUPS * B_CHUNK
C_WIDTH = C_HEADS * C_DIM
D_GROUP_WIDTH = D_HEADS * HEAD_DIM
D_WIDTH = len(D_PATTERNS) * D_GROUP_WIDTH
COL_QA = 0
COL_KA = COL_QA + A_WIDTH
COL_VA = COL_KA + A_KV_WIDTH
COL_UB = COL_VA + A_KV_WIDTH
COL_VB = COL_UB + B_WIDTH
COL_QC = COL_VB + B_WIDTH
COL_FC = COL_QC + C_WIDTH
COL_IC = COL_FC + C_WIDTH
COL_GC = COL_IC + C_WIDTH
COL_QD = COL_GC + C_WIDTH
COL_KD = COL_QD + D_WIDTH
COL_VD = COL_KD + D_WIDTH
MIX_WIDTH = COL_VD + D_WIDTH

EPILOGUE_PANEL = 256
FFN_RING_SLOTS = 3
ATTN_UNROLL = 8
HGRN_CHUNK = 128
HGRN_LEVELS = (64, 32, 16, 8, 4, 2)


def _cparams(*sem):
    return pltpu.CompilerParams(dimension_semantics=sem, vmem_limit_bytes=VMEM_LIMIT_BYTES)


def _layer_norm(v, g, b):
    mu = jnp.mean(v, axis=-1, keepdims=True)
    d = v - mu
    var = jnp.mean(d * d, axis=-1, keepdims=True)
    return d * lax.rsqrt(var + LN_EPS) * g + b


def _sigmoid(x):
    return 0.5 * jnp.tanh(0.5 * x) + 0.5


def _dot(a, b):
    return jnp.dot(a, b, preferred_element_type=F32)


def _dot_nt(a, b):
    return lax.dot_general(a, b, (((1,), (1,)), ((), ())), preferred_element_type=F32)


def _dot_tn(a, b):
    return lax.dot_general(a, b, (((0,), (0,)), ((), ())), preferred_element_type=F32)


def _ffn_kernel(x_ref, wg_hbm, wu_hbm, wd_hbm, g_ref, b_ref, *rest, alpha, has_extra, layer, which, nt, nf, tf):
    ahead = FFN_RING_SLOTS - 1
    if has_extra:
        e_ref, o_ref, ob_ref, xb_ref, acc_ref, wgu_buf, wd_buf, sem = rest
    else:
        o_ref, ob_ref, xb_ref, acc_ref, wgu_buf, wd_buf, sem = rest
    i = pl.program_id(0)
    base = i * nf

    def copies(pos):
        slot = pos % FFN_RING_SLOTS
        off = (pos % nf) * tf
        if not isinstance(off, int):
            off = pl.multiple_of(off, tf)
        gate_dst = wgu_buf.at[slot, :, pl.ds(0, tf)]
        up_dst = wgu_buf.at[slot, :, pl.ds(tf, tf)]
        return (
            pltpu.make_async_copy(wg_hbm.at[layer, which, :, pl.ds(off, tf)], gate_dst, sem.at[0, slot]),
            pltpu.make_async_copy(wu_hbm.at[layer, which, :, pl.ds(off, tf)], up_dst, sem.at[1, slot]),
            pltpu.make_async_copy(wd_hbm.at[layer, which, pl.ds(off, tf), :], wd_buf.at[slot], sem.at[2, slot]),
        )

    def fetch(pos):
        for c in copies(pos):
            c.wait()

        @pl.when(pos + ahead < nt * nf)
        def _():
            for c in copies(pos + ahead):
                c.start()

    def contribution(pos):
        slot = pos % FFN_RING_SLOTS
        gu = _dot(xb_ref[...], wgu_buf[slot])
        g, u = gu[:, :tf], gu[:, tf:]
        half_g = 0.5 * g
        h = (half_g * (jnp.tanh(half_g) + 1.0) * u).astype(BF16)
        return _dot(h, wd_buf[slot])

    def finish_previous():
        y = 0.5 * acc_ref[...]
        if has_extra:
            y = y + e_ref[...]
        o = _layer_norm(y, g_ref[...], b_ref[...])
        o_ref[...] = o
        ob_ref[...] = o.astype(BF16)

    @pl.when(i == 0)
    def _():
        acc_ref[...] = jnp.zeros_like(acc_ref)
        for pos in range(min(ahead, nt * nf)):
            for c in copies(pos):
                c.start()

    @pl.when(i < nt)
    def _():
        fetch(base)
        finish_previous()
        xb_ref[...] = x_ref[...].astype(BF16)
        acc_ref[...] = (2.0 * alpha) * x_ref[...] + contribution(base)

        def chunk(f, carry):
            fetch(base + f)
            acc_ref[...] = acc_ref[...] + contribution(base + f)
            return carry

        lax.fori_loop(1, nf, chunk, 0)

    @pl.when(i == nt)
    def _():
        finish_previous()


def _ffn(x, wg, wu, wd, layer, which, ln_g, ln_b, extra, alpha, tm=512, tf=512):
    t, d = x.shape
    ff = wg.shape[-1]
    tm = min(tm, t)
    tf = min(tf, ff)
    nt, nf = t // tm, ff // tf
    cur = pl.BlockSpec((tm, d), lambda i: (jnp.minimum(i, nt - 1), 0))
    lag = pl.BlockSpec((tm, d), lambda i: (jnp.maximum(i - 1, 0), 0))
    vec = pl.BlockSpec((1, d), lambda i: (0, 0))
    hbm = pl.BlockSpec(memory_space=pl.ANY)
    in_specs = [cur, hbm, hbm, hbm, vec, vec]
    args = [x, wg, wu, wd, ln_g, ln_b]
    if extra is not None:
        in_specs.append(lag)
        args.append(extra)
    return pl.pallas_call(
        functools.partial(_ffn_kernel, alpha=alpha, has_extra=extra is not None, layer=layer, which=which, nt=nt,
                          nf=nf, tf=tf),
        grid=(nt + 1,),
        in_specs=in_specs,
        out_specs=[lag, lag],
        out_shape=[jax.ShapeDtypeStruct((t, d), F32), jax.ShapeDtypeStruct((t, d), BF16)],
        scratch_shapes=[pltpu.VMEM((tm, d), BF16), pltpu.VMEM((tm, d), F32),
                        pltpu.VMEM((FFN_RING_SLOTS, d, 2 * tf), BF16), pltpu.VMEM((FFN_RING_SLOTS, tf, d), BF16),
                        pltpu.SemaphoreType.DMA((3, FFN_RING_SLOTS))],
        compiler_params=_cparams("arbitrary"),
        name="ffn_ln",
    )(*args)


def _proj_kernel(x_ref, w_ref, o_ref, *, act):
    z = _dot(x_ref[...], w_ref[...])
    if act:
        z = _sigmoid(z)
    o_ref[...] = z.astype(o_ref.dtype)


def _proj(xb, w, layer, col0, width, out_dtype, act):
    t, d = xb.shape
    tm = min(2048, t)
    tn = min(1024, width)
    off = col0 // tn
    return pl.pallas_call(
        functools.partial(_proj_kernel, act=act),
        grid=(width // tn, t // tm),
        in_specs=[
            pl.BlockSpec((tm, d), lambda n, m: (m, 0)),
            pl.BlockSpec((None, d, tn), lambda n, m: (layer, 0, n + off)),
        ],
        out_specs=pl.BlockSpec((tm, tn), lambda n, m: (m, n)),
        out_shape=jax.ShapeDtypeStruct((t, width), out_dtype),
        compiler_params=_cparams("parallel", "arbitrary"),
        name="in_proj_gates" if act else "in_proj_mix",
    )(xb, w)


def _band_attn_kernel(slope_ref, sink_ref, q_ref, k_ref, v_ref, kp_ref, vp_ref, *rest,
                      dil, nblk, max_dist, sb_per_seq, gqa, use_sinks, want_lse):
    if want_lse:
        o_ref, lse_ref, kbuf, vbuf, bias_ref = rest
    else:
        o_ref, kbuf, vbuf, bias_ref = rest
        lse_ref = None
    i = pl.program_id(0)
    p = pl.program_id(1)
    bw = BLOCK * dil
    first_in_seq = (i % sb_per_seq) == 0
    lane = lax.broadcasted_iota(jnp.int32, (1, LANES), 1)
    low = lane < HEAD_DIM

    def stage(cur_ref, prev_ref, buf):
        cur = cur_ref[...]
        prev = prev_ref[...]
        if gqa:
            kv_first = (p // 2) == 0

            def dup(a):
                r = pltpu.roll(a, HEAD_DIM, 1)
                return jnp.where(kv_first, jnp.where(low, a, r), jnp.where(low, r, a))

            cur, prev = dup(cur), dup(prev)
        buf[0:bw, :] = prev.astype(buf.dtype)
        buf[bw:, :] = cur.astype(buf.dtype)

    if gqa:
        @pl.when(p % 2 == 0)
        def _():
            stage(k_ref, kp_ref, kbuf)
            stage(v_ref, vp_ref, vbuf)
    else:
        stage(k_ref, kp_ref, kbuf)
        stage(v_ref, vp_ref, vbuf)

    q_off = lax.broadcasted_iota(jnp.int32, (2 * BLOCK, 2 * BLOCK), 0) % BLOCK
    k_off = lax.broadcasted_iota(jnp.int32, (2 * BLOCK, 2 * BLOCK), 1)
    dist = q_off + BLOCK - k_off
    in_band = (dist >= 0) & (dist <= max_dist)
    is_cur = k_off >= BLOCK
    dist_f = (jnp.maximum(dist, 0) * dil).astype(F32)
    top = lax.broadcasted_iota(jnp.int32, (2 * BLOCK, 1), 0) < BLOCK
    neg_slope = jnp.where(top, -slope_ref[2 * p], -slope_ref[2 * p + 1])
    bias = jnp.where(in_band, neg_slope * dist_f, NEG_BIG)
    bias_ref[0] = bias
    bias_ref[1] = jnp.where(is_cur, bias, NEG_BIG)
    if use_sinks:
        sink = jnp.where(top, sink_ref[2 * p], sink_ref[2 * p + 1])
    ones = jnp.ones((2 * BLOCK, LANES), BF16)

    def body(it, carry):
        j = it // dil
        c = it % dil
        row0 = j * bw + c
        if dil == 1:
            row0 = pl.multiple_of(row0, BLOCK)
            qs = pl.ds(row0, BLOCK)
            ks = pl.ds(row0, 2 * BLOCK)
        else:
            qs = pl.ds(row0, BLOCK, stride=dil)
            ks = pl.ds(row0, 2 * BLOCK, stride=dil)
        qc = q_ref[qs, :] * (HEAD_DIM ** -0.5)
        kk = kbuf[ks, :].astype(BF16)
        vv = vbuf[ks, :].astype(BF16)
        q2 = jnp.concatenate([jnp.where(low, qc, 0.0), jnp.where(low, 0.0, qc)], axis=0).astype(BF16)
        tbl = bias_ref[jnp.logical_and(j == 0, first_in_seq).astype(jnp.int32)]
        s = jnp.where(tbl > 0.5 * NEG_BIG, _dot_nt(q2, kk) + tbl, NEG_BIG)
        m = jnp.max(s, axis=-1, keepdims=True)
        if use_sinks:
            m = jnp.maximum(m, sink)
        prb = jnp.exp(s - m).astype(BF16)
        ov = _dot(prb, jnp.concatenate([vv, ones], axis=1))
        pick = lambda a: jnp.where(low, a[0:BLOCK], a[BLOCK:])
        den = pick(ov[:, LANES:])
        if use_sinks:
            den = den + pick(jnp.exp(sink - m))
        o_ref[qs, :] = (pick(ov[:, :LANES]) / den).astype(o_ref.dtype)
        if want_lse:
            lse_ref[qs, :] = pick(m) + jnp.log(den)
        return carry

    lax.fori_loop(0, nblk * dil, body, 0, unroll=ATTN_UNROLL)


def _band_attn(z, slopes, sinks, *, seq, q_col, k_col, v_col, n_pairs, dil, max_dist, gqa, use_sinks,
               want_lse, out_dtype):
    t = z.shape[0]
    bw = BLOCK * dil
    sb = min(max(2 * bw, 2048), seq)
    nblk = sb // bw
    qb, kb, vb = q_col // LANES, k_col // LANES, v_col // LANES
    if gqa:
        kv_map = lambda i, p: (i, kb)
        vv_map = lambda i, p: (i, vb)
        kp_map = lambda i, p: (jnp.maximum(i * nblk - 1, 0), kb)
        vp_map = lambda i, p: (jnp.maximum(i * nblk - 1, 0), vb)
    else:
        kv_map = lambda i, p: (i, kb + p)
        vv_map = lambda i, p: (i, vb + p)
        kp_map = lambda i, p: (jnp.maximum(i * nblk - 1, 0), kb + p)
        vp_map = lambda i, p: (jnp.maximum(i * nblk - 1, 0), vb + p)
    smem = pl.BlockSpec(memory_space=pltpu.SMEM)
    out_spec = pl.BlockSpec((sb, LANES), lambda i, p: (i, p))
    out_specs = [out_spec]
    out_shape = [jax.ShapeDtypeStruct((t, n_pairs * LANES), out_dtype)]
    if want_lse:
        out_specs.append(out_spec)
        out_shape.append(jax.ShapeDtypeStruct((t, n_pairs * LANES), F32))
    return pl.pallas_call(
        functools.partial(_band_attn_kernel, dil=dil, nblk=nblk, max_dist=max_dist, sb_per_seq=seq // sb,
                          gqa=gqa, use_sinks=use_sinks, want_lse=want_lse),
        grid=(t // sb, n_pairs),
        in_specs=[
            smem,
            smem,
            pl.BlockSpec((sb, LANES), lambda i, p: (i, qb + p)),
            pl.BlockSpec((sb, LANES), kv_map),
            pl.BlockSpec((sb, LANES), vv_map),
            pl.BlockSpec((bw, LANES), kp_map),
            pl.BlockSpec((bw, LANES), vp_map),
        ],
        out_specs=out_specs,
        out_shape=out_shape,
        scratch_shapes=[pltpu.VMEM((bw + sb, LANES), BF16 if dil == 1 else F32)] * 2
        + [pltpu.VMEM((2, 2 * BLOCK, 2 * BLOCK), F32)],
        compiler_params=_cparams("parallel", "arbitrary"),
        name=f"band_attn_d{dil}" + ("_gqa" if gqa else ""),
    )(slopes, sinks, z, z, z, z, z)


def _gelu(x):
    return 0.5 * x * (1.0 + lax.erf(x * np.float32(np.sqrt(0.5))))


def _gmlp_kernel(u0_ref, u1_ref, v0_ref, v1_ref, g_ref, b_ref, ws_ref, bs_ref, o_ref):
    rows = u0_ref.shape[0]
    u = _gelu(jnp.concatenate([u0_ref[...], u1_ref[...]], axis=1))
    v = _gelu(jnp.concatenate([v0_ref[...], v1_ref[...]], axis=1))
    v = _layer_norm(v, g_ref[...], b_ref[...]).astype(BF16)
    t_idx = lax.broadcasted_iota(jnp.int32, (B_CHUNK, B_CHUNK), 0)
    s_idx = lax.broadcasted_iota(jnp.int32, (B_CHUNK, B_CHUNK), 1)
    causal = t_idx >= s_idx
    for grp in range(B_GROUPS):
        w = jnp.where(causal, ws_ref[grp], 0.0).astype(BF16)
        bias = bs_ref[:, grp:grp + 1]
        cols = slice(grp * B_CHUNK, (grp + 1) * B_CHUNK)
        for ch in range(rows // B_CHUNK):
            rs = slice(ch * B_CHUNK, (ch + 1) * B_CHUNK)
            mixed = _dot(w, v[rs, cols]) + bias
            o_ref[rs, cols] = (u[rs, cols] * mixed).astype(o_ref.dtype)


def _gmlp(z, ln_g, ln_b, w_s, b_s_t):
    t = z.shape[0]
    rows = min(2048, t)
    half = B_WIDTH // 2
    ub, vb = COL_UB // half, COL_VB // half
    col = lambda j: pl.BlockSpec((rows, half), lambda i: (i, j))
    vec = pl.BlockSpec((1, B_WIDTH), lambda i: (0, 0))
    return pl.pallas_call(
        _gmlp_kernel,
        grid=(t // rows,),
        in_specs=[col(ub), col(ub + 1), col(vb), col(vb + 1), vec, vec,
                  pl.BlockSpec((B_GROUPS, B_CHUNK, B_CHUNK), lambda i: (0, 0, 0)),
                  pl.BlockSpec((B_CHUNK, B_GROUPS), lambda i: (0, 0))],
        out_specs=pl.BlockSpec((rows, B_WIDTH), lambda i: (i, 0)),
        out_shape=jax.ShapeDtypeStruct((t, B_WIDTH), BF16),
        compiler_params=_cparams("parallel"),
        name="spatial_gating",
    )(z, z, z, z, ln_g, ln_b, w_s, b_s_t)


def _hgrn_tables():
    n = HGRN_CHUNK
    t = np.arange(n)[:, None]
    i = np.arange(n)[None, :]
    tabs = []
    for h in HGRN_LEVELS:
        start = (t // h) * h
        upper = ((t // h) % 2) == 1
        q_tab = (i >= start) & (i <= t)
        k_tab = (i > t) & (i < start + h)
        tabs.append(np.where(upper, q_tab, k_tab))
    tabs.append(i <= t)
    tabs.append(i > t)
    w = np.concatenate(tabs, axis=0).astype(np.float32)
    w = np.concatenate([w, w], axis=1)
    s = np.arange(n)[None, :]
    x = np.bitwise_xor(t, s)
    level = np.where(s < t, np.floor(np.log2(np.maximum(x, 1))).astype(np.int32), np.where(s == t, -1, -2))
    return w, level.astype(np.int32)


def _hgrn_kernel(lbl_ref, ng_ref, w_ref, lvl_ref, q0_ref, q1_ref, f0_ref, f1_ref, i0_ref, i1_ref, g0_ref, g1_ref,
                 o_ref, st_ref, *, layer):
    n = HGRN_CHUNK

    @pl.when(pl.program_id(1) == 0)
    def _():
        st_ref[...] = jnp.zeros_like(st_ref)

    logits = lbl_ref[...]
    e = jnp.exp(logits - jnp.max(logits, axis=0, keepdims=True))
    probs = e / jnp.sum(e, axis=0, keepdims=True)
    lb = jnp.zeros((1, C_WIDTH), F32)
    for j in range(1, layer + 1):
        lb = lb + probs[j:j + 1, :]

    w = w_ref[...]
    lvl = lvl_ref[...]
    row = lax.broadcasted_iota(jnp.int32, (n, C_WIDTH), 0)
    ng = ng_ref[...]
    nl = len(HGRN_LEVELS)
    for ch in range(q0_ref.shape[0] // n):
        rs = slice(ch * n, (ch + 1) * n)
        both = lambda a_ref, b_ref: jnp.concatenate([a_ref[rs, :], b_ref[rs, :]], axis=1)
        z = both(f0_ref, f1_ref)
        q = both(q0_ref, q1_ref)
        f = lb + (1.0 - lb) * jax.nn.sigmoid(z)
        logf = jnp.log(jnp.maximum(f, F_MIN))
        kk = (1.0 - lb) * jax.nn.sigmoid(-z)
        vb = both(i0_ref, i1_ref).astype(BF16)
        l1 = logf.astype(BF16)
        l2 = (logf - l1.astype(F32)).astype(BF16)
        expo = _dot(w, jnp.concatenate([l1, l2], axis=0))
        xs = []
        for li, h in enumerate(HGRN_LEVELS):
            upper = (row & h) != 0
            xs.append((jnp.where(upper, q, kk) * jnp.exp(expo[li * n:(li + 1) * n])).astype(BF16))
        x1 = jnp.where((row & 1) != 0, q * jnp.maximum(f, F_MIN), kk).astype(BF16)
        q_in = (q * jnp.exp(expo[nl * n:(nl + 1) * n])).astype(BF16)
        k_out = (kk * jnp.exp(expo[(nl + 1) * n:(nl + 2) * n])).astype(BF16)
        carry = jnp.exp(expo[(nl + 1) * n - 1:(nl + 1) * n])
        qb, kb = q.astype(BF16), kk.astype(BF16)
        outs = []
        for hd in range(C_HEADS):
            cs = slice(hd * C_DIM, (hd + 1) * C_DIM)
            scores = jnp.where(lvl == -1, _dot_nt(qb[:, cs], kb[:, cs]), 0.0)
            scores = scores + jnp.where(lvl == 0, _dot_nt(x1[:, cs], x1[:, cs]), 0.0)
            for li, h in enumerate(HGRN_LEVELS):
                scores = scores + jnp.where(lvl == int(np.log2(h)), _dot_nt(xs[li][:, cs], xs[li][:, cs]), 0.0)
            st = st_ref[hd]
            o = _dot(scores.astype(BF16), vb[:, cs]) + _dot_nt(q_in[:, cs], st.astype(BF16))
            st_ref[hd] = st * carry[:, cs] + _dot_tn(vb[:, cs], k_out[:, cs])
            outs.append(o * lax.rsqrt(jnp.mean(o * o, axis=-1, keepdims=True) + LN_EPS))
        o = jnp.concatenate(outs, axis=1) * ng
        o_ref[rs, :] = (o * _sigmoid(both(g0_ref, g1_ref))).astype(o_ref.dtype)


def _hgrn(z, lb_logits, norm_g, layer, batch, seq):
    t = z.shape[0]
    rows = min(1024, seq)
    per_seq = seq // rows
    half = C_WIDTH // 2
    w, lvl = _hgrn_tables()
    col = lambda c0, j: pl.BlockSpec((rows, half), lambda b, c: (b * per_seq + c, c0 // half + j))
    const = lambda shape: pl.BlockSpec(shape, lambda b, c: (0,) * len(shape))
    return pl.pallas_call(
        functools.partial(_hgrn_kernel, layer=layer),
        grid=(batch, per_seq),
        in_specs=[const(lb_logits.shape), const(norm_g.shape), const(w.shape), const(lvl.shape)]
        + [col(c0, j) for c0 in (COL_QC, COL_FC, COL_IC, COL_GC) for j in (0, 1)],
        out_specs=pl.BlockSpec((rows, C_WIDTH), lambda b, c: (b * per_seq + c, 0)),
        out_shape=jax.ShapeDtypeStruct((t, C_WIDTH), BF16),
        scratch_shapes=[pltpu.VMEM((C_HEADS, C_DIM, C_DIM), F32)],
        compiler_params=_cparams("parallel", "arbitrary"),
        name="hgrn2",
    )(lb_logits, norm_g, jnp.asarray(w, BF16), jnp.asarray(lvl), *([z] * 8))


def _merge_kernel(x_ref, oa_ref, ob_ref, oc_ref, d0_ref, d1_ref, d2_ref, l0_ref, l1_ref, l2_ref, gt_ref,
                  wa_ref, wb_ref, wc_ref, wd_ref, wo_ref, g_ref, b_ref, o_ref, obf_ref, y_ref, *, alpha):
    d = x_ref.shape[1]
    i = pl.program_id(0)
    nt = pl.num_programs(0) - 1

    def finish_previous():
        o = _layer_norm(y_ref[...], g_ref[...], b_ref[...])
        o_ref[...] = o
        obf_ref[...] = o.astype(BF16)

    @pl.when(i == 0)
    def _():
        y_ref[...] = jnp.zeros_like(y_ref)

    @pl.when(i < nt)
    def _():
        finish_previous()
        l0, l1, l2 = l0_ref[...], l1_ref[...], l2_ref[...]
        m = jnp.maximum(jnp.maximum(l0, l1), l2)
        e0, e1, e2 = jnp.exp(l0 - m), jnp.exp(l1 - m), jnp.exp(l2 - m)
        inv = 1.0 / (e0 + e1 + e2)
        od = (e0 * inv) * d0_ref[...] + (e1 * inv) * d1_ref[...] + (e2 * inv) * d2_ref[...]
        merged = gt_ref[:, 0:d].astype(F32) * _dot(oa_ref[...], wa_ref[...])
        merged = merged + gt_ref[:, d:2 * d].astype(F32) * _dot(ob_ref[...], wb_ref[...])
        merged = merged + gt_ref[:, 2 * d:3 * d].astype(F32) * _dot(oc_ref[...], wc_ref[...])
        merged = merged + gt_ref[:, 3 * d:4 * d].astype(F32) * _dot(od.astype(BF16), wd_ref[...])
        y_ref[...] = alpha * x_ref[...] + _dot(merged.astype(BF16), wo_ref[...])

    @pl.when(i == nt)
    def _():
        finish_previous()


def _merge(x, oa, ob, oc, ods, lses, gates, wa, wb, wc, wd, wo, layer, ln_g, ln_b, alpha):
    t, d = x.shape
    tm = min(256, t)
    nt = t // tm
    row = lambda width: pl.BlockSpec((tm, width), lambda i: (jnp.minimum(i, nt - 1), 0))
    lag = pl.BlockSpec((tm, d), lambda i: (jnp.maximum(i - 1, 0), 0))
    wspec = lambda k: pl.BlockSpec((None, k, d), lambda i: (layer, 0, 0), pipeline_mode=pl.Buffered(1))
    vec = pl.BlockSpec((1, d), lambda i: (0, 0))
    return pl.pallas_call(
        functools.partial(_merge_kernel, alpha=alpha),
        grid=(nt + 1,),
        in_specs=[row(d), row(A_WIDTH), row(B_WIDTH), row(C_WIDTH)] + [row(D_GROUP_WIDTH)] * 6
        + [row(N_BRANCHES * d), wspec(A_WIDTH), wspec(B_WIDTH), wspec(C_WIDTH), wspec(D_GROUP_WIDTH), wspec(d),
           vec, vec],
        out_specs=[lag, lag],
        out_shape=[jax.ShapeDtypeStruct((t, d), F32), jax.ShapeDtypeStruct((t, d), BF16)],
        scratch_shapes=[pltpu.VMEM((tm, d), F32)],
        compiler_params=_cparams("arbitrary"),
        name="merge_out_ln",
    )(x, oa, ob, oc, *ods, *lses, gates, wa, wb, wc, wd, wo, ln_g, ln_b)


def _ple_kernel(xb_ref, p_ref, wg_ref, wp_ref, o_ref):
    pb = p_ref[...].astype(BF16)
    for c0 in range(0, o_ref.shape[1], EPILOGUE_PANEL):
        cs = slice(c0, c0 + EPILOGUE_PANEL)
        o_ref[:, cs] = _sigmoid(_dot(xb_ref[...], wg_ref[:, cs])) * _dot(pb, wp_ref[:, cs])


def _ple(xb, p, wg, wp, layer):
    t, d = xb.shape
    pd = p.shape[-1]
    tm = min(1024, t)
    tn = min(1024, d)
    return pl.pallas_call(
        _ple_kernel,
        grid=(d // tn, t // tm),
        in_specs=[
            pl.BlockSpec((tm, d), lambda n, m: (m, 0)),
            pl.BlockSpec((None, tm, pd), lambda n, m: (layer, m, 0)),
            pl.BlockSpec((None, d, tn), lambda n, m: (layer, 0, n)),
            pl.BlockSpec((None, pd, tn), lambda n, m: (layer, 0, n)),
        ],
        out_specs=pl.BlockSpec((tm, tn), lambda n, m: (m, n)),
        out_shape=jax.ShapeDtypeStruct((t, d), F32),
        compiler_params=_cparams("parallel", "arbitrary"),
        name="ple_gate",
    )(xb, p, wg, wp)


def _alibi_slopes(n):
    return jnp.exp2(-8.0 * jnp.arange(1, n + 1, dtype=F32) / n)


def kernel(x, p, ln_g, ln_b, ffn_w_gate, ffn_w_up, ffn_w_down, w_in, attn_sinks, gmlp_ln_g, gmlp_ln_b, gmlp_w_s,
           gmlp_b_s, hgrn_lb_logits, hgrn_norm_g, w_br_a, w_br_b, w_br_c, w_br_d, w_out, ple_w_proj, ple_w_gate):
    batch, seq, d = x.shape
    depth = ln_g.shape[0]
    t = batch * seq
    alpha = (2.0 * depth) ** 0.25
    assert seq % (BLOCK * D_PATTERNS[-1][1]) == 0, "the widest dilation needs whole 128-row residue blocks"
    assert w_in.shape[-1] == MIX_WIDTH + N_BRANCHES * d
    bf =lambda a: a.astype(BF16)
    wg, wu, wd = bf(ffn_w_gate), bf(ffn_w_up), bf(ffn_w_down)
    w_in_b = bf(w_in)
    wa, wb, wc, wdd, wo = bf(w_br_a), bf(w_br_b), bf(w_br_c), bf(w_br_d), bf(w_out)
    wpp, wpg = bf(ple_w_proj), bf(ple_w_gate)
    p2 = p.reshape(depth, t, p.shape[-1])
    slopes = _alibi_slopes(N_SOFTMAX_HEADS)
    no_sinks = jnp.zeros((D_HEADS,), F32)

    xf = x.reshape(t, d)
    for i in range(depth):
        xf, xb = _ffn(xf, wg, wu, wd, i, 0, ln_g[i, 0][None], ln_b[i, 0][None], None, alpha)
        z = _proj(xb, w_in_b, i, 0, MIX_WIDTH, F32, False)
        gates = _proj(xb, w_in_b, i, MIX_WIDTH, N_BRANCHES * d, BF16, True)
        oa = _band_attn(z, slopes[:A_Q_HEADS], attn_sinks[i], seq=seq, q_col=COL_QA, k_col=COL_KA, v_col=COL_VA,
                        n_pairs=A_Q_HEADS // 2, dil=1, max_dist=A_WINDOW - 1, gqa=True, use_sinks=True,
                        want_lse=False, out_dtype=BF16)[0]
        ob = _gmlp(z, gmlp_ln_g[i][None], gmlp_ln_b[i][None], gmlp_w_s[i], gmlp_b_s[i].T)
        oc = _hgrn(z, hgrn_lb_logits, hgrn_norm_g[i][None], i, batch, seq)
        ods, lses = [], []
        for g, (window, dil) in enumerate(D_PATTERNS):
            h0 = A_Q_HEADS + g * D_HEADS
            od, lse = _band_attn(z, slopes[h0:h0 + D_HEADS], no_sinks, seq=seq, q_col=COL_QD + g * D_GROUP_WIDTH,
                                 k_col=COL_KD + g * D_GROUP_WIDTH, v_col=COL_VD + g * D_GROUP_WIDTH,
                                 n_pairs=D_HEADS // 2, dil=dil, max_dist=window // dil, gqa=False, use_sinks=False,
                                 want_lse=True, out_dtype=F32)
            ods.append(od)
            lses.append(lse)
        xf, xb = _merge(xf, oa, ob, oc, ods, lses, gates, wa, wb, wc, wdd, wo, i, ln_g[i, 1][None], ln_b[i, 1][None],
                        alpha)
        ple = _ple(xb, p2, wpg, wpp, i)
        xf, xb = _ffn(xf, wg, wu, wd, i, 1, ln_g[i, 2][None], ln_b[i, 2][None], ple, alpha)
    return xf.reshape(batch, seq, d)
```

```python
import functools

import numpy as np

import jax
import jax.numpy as jnp
from jax import lax
from jax.experimental import pallas as pl
from jax.experimental.pallas import tpu as pltpu

F32 = jnp.float32
BF16 = jnp.bfloat16

LANES = 128
VMEM_LIMIT_BYTES = 60 * 1024 * 1024

HEAD_DIM = 64
BLOCK = 128
LN_EPS = 1e-5
NEG_BIG = -1e30
F_MIN = 1e-6
A_Q_HEADS = 8
A_KV_HEADS = 2
A_WINDOW = 128
B_GROUPS = 4
B_CHUNK = 128
C_HEADS = 4
C_DIM = 128
D_PATTERNS = ((128, 1), (512, 4), (2048, 16))
D_HEADS = 4
N_SOFTMAX_HEADS = A_Q_HEADS + len(D_PATTERNS) * D_HEADS
N_BRANCHES = 4
A_WIDTH = A_Q_HEADS * HEAD_DIM
A_KV_WIDTH = A_KV_HEADS * HEAD_DIM
B_WIDTH = B_GROUPS * B_CHUNK
C_WIDTH = C_HEADS * C_DIM
D_GROUP_WIDTH = D_HEADS * HEAD_DIM
D_WIDTH = len(D_PATTERNS) * D_GROUP_WIDTH
COL_QA = 0
COL_KA = COL_QA + A_WIDTH
COL_VA = COL_KA + A_KV_WIDTH
COL_UB = COL_VA + A_KV_WIDTH
COL_VB = COL_UB + B_WIDTH
COL_QC = COL_VB + B_WIDTH
COL_FC = COL_QC + C_WIDTH
COL_IC = COL_FC + C_WIDTH
COL_GC = COL_IC + C_WIDTH
COL_QD = COL_GC + C_WIDTH
COL_KD = COL_QD + D_WIDTH
COL_VD = COL_KD + D_WIDTH
MIX_WIDTH = COL_VD + D_WIDTH

EPILOGUE_PANEL = 256
FFN_RING_SLOTS = 3
ATTN_UNROLL = 16
HGRN_CHUNK = 128
HGRN_LEVELS = (64, 32, 16, 8, 4, 2)


def _cparams(*sem):
    return pltpu.CompilerParams(dimension_semantics=sem, vmem_limit_bytes=VMEM_LIMIT_BYTES)


def _layer_norm(v, g, b):
    mu = jnp.mean(v, axis=-1, keepdims=True)
    d = v - mu
    var = jnp.mean(d * d, axis=-1, keepdims=True)
    return d * lax.rsqrt(var + LN_EPS) * g + b


def _sigmoid(x):
    return 0.5 * jnp.tanh(0.5 * x) + 0.5


def _dot(a, b):
    return jnp.dot(a, b, preferred_element_type=F32)


def _dot_nt(a, b):
    return lax.dot_general(a, b, (((1,), (1,)), ((), ())), preferred_element_type=F32)


def _dot_tn(a, b):
    return lax.dot_general(a, b, (((0,), (0,)), ((), ())), preferred_element_type=F32)


def _ffn_kernel(x_ref, wg_hbm, wu_hbm, wd_hbm, g_ref, b_ref, *rest, alpha, has_extra, layer, which, nt, nf, tf):
    ahead = FFN_RING_SLOTS - 1
    if has_extra:
        e_ref, o_ref, ob_ref, xb_ref, acc_ref, wgu_buf, wd_buf, sem = rest
    else:
        o_ref, ob_ref, xb_ref, acc_ref, wgu_buf, wd_buf, sem = rest
    i = pl.program_id(0)
    base = i * nf

    def copies(pos):
        slot = pos % FFN_RING_SLOTS
        off = (pos % nf) * tf
        if not isinstance(off, int):
            off = pl.multiple_of(off, tf)
        gate_dst = wgu_buf.at[slot, :, pl.ds(0, tf)]
        up_dst = wgu_buf.at[slot, :, pl.ds(tf, tf)]
        return (
            pltpu.make_async_copy(wg_hbm.at[layer, which, :, pl.ds(off, tf)], gate_dst, sem.at[0, slot]),
            pltpu.make_async_copy(wu_hbm.at[layer, which, :, pl.ds(off, tf)], up_dst, sem.at[1, slot]),
            pltpu.make_async_copy(wd_hbm.at[layer, which, pl.ds(off, tf), :], wd_buf.at[slot], sem.at[2, slot]),
        )

    def fetch(pos):
        for c in copies(pos):
            c.wait()

        @pl.when(pos + ahead < nt * nf)
        def _():
            for c in copies(pos + ahead):
                c.start()

    def contribution(pos):
        slot = pos % FFN_RING_SLOTS
        gu = _dot(xb_ref[...], wgu_buf[slot])
        g, u = gu[:, :tf], gu[:, tf:]
        half_g = 0.5 * g
        h = (half_g * (jnp.tanh(half_g) + 1.0) * u).astype(BF16)
        return _dot(h, wd_buf[slot])

    def finish_previous():
        y = 0.5 * acc_ref[...]
        if has_extra:
            y = y + e_ref[...]
        o = _layer_norm(y, g_ref[...], b_ref[...])
        o_ref[...] = o
        ob_ref[...] = o.astype(BF16)

    @pl.when(i == 0)
    def _():
        acc_ref[...] = jnp.zeros_like(acc_ref)
        for pos in range(min(ahead, nt * nf)):
            for c in copies(pos):
                c.start()

    @pl.when(i < nt)
    def _():
        fetch(base)
        finish_previous()
        xb_ref[...] = x_ref[...].astype(BF16)
        acc_ref[...] = (2.0 * alpha) * x_ref[...] + contribution(base)

        def chunk(f, carry):
            fetch(base + f)
            acc_ref[...] = acc_ref[...] + contribution(base + f)
            return carry

        lax.fori_loop(1, nf, chunk, 0)

    @pl.when(i == nt)
    def _():
        finish_previous()


def _ffn(x, wg, wu, wd, layer, which, ln_g, ln_b, extra, alpha, tm=512, tf=512):
    t, d = x.shape
    ff = wg.shape[-1]
    tm = min(tm, t)
    tf = min(tf, ff)
    nt, nf = t // tm, ff // tf
    cur = pl.BlockSpec((tm, d), lambda i: (jnp.minimum(i, nt - 1), 0))
    lag = pl.BlockSpec((tm, d), lambda i: (jnp.maximum(i - 1, 0), 0))
    vec = pl.BlockSpec((1, d), lambda i: (0, 0))
    hbm = pl.BlockSpec(memory_space=pl.ANY)
    in_specs = [cur, hbm, hbm, hbm, vec, vec]
    args = [x, wg, wu, wd, ln_g, ln_b]
    if extra is not None:
        in_specs.append(lag)
        args.append(extra)
    return pl.pallas_call(
        functools.partial(_ffn_kernel, alpha=alpha, has_extra=extra is not None, layer=layer, which=which, nt=nt,
                          nf=nf, tf=tf),
        grid=(nt + 1,),
        in_specs=in_specs,
        out_specs=[lag, lag],
        out_shape=[jax.ShapeDtypeStruct((t, d), F32), jax.ShapeDtypeStruct((t, d), BF16)],
        scratch_shapes=[pltpu.VMEM((tm, d), BF16), pltpu.VMEM((tm, d), F32),
                        pltpu.VMEM((FFN_RING_SLOTS, d, 2 * tf), BF16), pltpu.VMEM((FFN_RING_SLOTS, tf, d), BF16),
                        pltpu.SemaphoreType.DMA((3, FFN_RING_SLOTS))],
        compiler_params=_cparams("arbitrary"),
        name="ffn_ln",
    )(*args)


def _proj_kernel(x_ref, w_ref, o_ref, *, act):
    z = _dot(x_ref[...], w_ref[...])
    if act:
        z = _sigmoid(z)
    o_ref[...] = z.astype(o_ref.dtype)


def _proj(xb, w, layer, col0, width, out_dtype, act):
    t, d = xb.shape
    tm = min(2048, t)
    tn = min(1024, width)
    off = col0 // tn
    return pl.pallas_call(
        functools.partial(_proj_kernel, act=act),
        grid=(width // tn, t // tm),
        in_specs=[
            pl.BlockSpec((tm, d), lambda n, m: (m, 0)),
            pl.BlockSpec((None, d, tn), lambda n, m: (layer, 0, n + off)),
        ],
        out_specs=pl.BlockSpec((tm, tn), lambda n, m: (m, n)),
        out_shape=jax.ShapeDtypeStruct((t, width), out_dtype),
        compiler_params=_cparams("parallel", "arbitrary"),
        name="in_proj_gates" if act else "in_proj_mix",
    )(xb, w)


def _band_attn_kernel(slope_ref, sink_ref, q_ref, k_ref, v_ref, kp_ref, vp_ref, *rest,
                      dil, nblk, max_dist, sb_per_seq, gqa, use_sinks, want_lse):
    if want_lse:
        o_ref, lse_ref, kbuf, vbuf, bias_ref = rest
    else:
        o_ref, kbuf, vbuf, bias_ref = rest
        lse_ref = None
    i = pl.program_id(0)
    p = pl.program_id(1)
    bw = BLOCK * dil
    first_in_seq = (i % sb_per_seq) == 0
    lane = lax.broadcasted_iota(jnp.int32, (1, LANES), 1)
    low = lane < HEAD_DIM

    def stage(cur_ref, prev_ref, buf):
        cur = cur_ref[...]
        prev = prev_ref[...]
        if gqa:
            kv_first = (p // 2) == 0

            def dup(a):
                r = pltpu.roll(a, HEAD_DIM, 1)
                return jnp.where(kv_first, jnp.where(low, a, r), jnp.where(low, r, a))

            cur, prev = dup(cur), dup(prev)
        buf[0:bw, :] = prev.astype(buf.dtype)
        buf[bw:, :] = cur.astype(buf.dtype)

    if gqa:
        @pl.when(p % 2 == 0)
        def _():
            stage(k_ref, kp_ref, kbuf)
            stage(v_ref, vp_ref, vbuf)
    else:
        stage(k_ref, kp_ref, kbuf)
        stage(v_ref, vp_ref, vbuf)

    q_off = lax.broadcasted_iota(jnp.int32, (2 * BLOCK, 2 * BLOCK), 0) % BLOCK
    k_off = lax.broadcasted_iota(jnp.int32, (2 * BLOCK, 2 * BLOCK), 1)
    dist = q_off + BLOCK - k_off
    in_band = (dist >= 0) & (dist <= max_dist)
    is_cur = k_off >= BLOCK
    dist_f = (jnp.maximum(dist, 0) * dil).astype(F32)
    top = lax.broadcasted_iota(jnp.int32, (2 * BLOCK, 1), 0) < BLOCK
    neg_slope = jnp.where(top, -slope_ref[2 * p], -slope_ref[2 * p + 1])
    bias = jnp.where(in_band, neg_slope * dist_f, NEG_BIG)
    bias_ref[0] = bias
    bias_ref[1] = jnp.where(is_cur, bias, NEG_BIG)
    if use_sinks:
        sink = jnp.where(top, sink_ref[2 * p], sink_ref[2 * p + 1])
    ones = jnp.ones((2 * BLOCK, LANES), BF16)

    def body(it, carry):
        j = it // dil
        c = it % dil
        row0 = j * bw + c
        if dil == 1:
            row0 = pl.multiple_of(row0, BLOCK)
            qs = pl.ds(row0, BLOCK)
            ks = pl.ds(row0, 2 * BLOCK)
        else:
            qs = pl.ds(row0, BLOCK, stride=dil)
            ks = pl.ds(row0, 2 * BLOCK, stride=dil)
        qc = q_ref[qs, :] * (HEAD_DIM ** -0.5)
        kk = kbuf[ks, :].astype(BF16)
        vv = vbuf[ks, :].astype(BF16)
        q2 = jnp.concatenate([jnp.where(low, qc, 0.0), jnp.where(low, 0.0, qc)], axis=0).astype(BF16)
        tbl = bias_ref[jnp.logical_and(j == 0, first_in_seq).astype(jnp.int32)]
        s = jnp.where(tbl > 0.5 * NEG_BIG, _dot_nt(q2, kk) + tbl, NEG_BIG)
        m = jnp.max(s, axis=-1, keepdims=True)
        if use_sinks:
            m = jnp.maximum(m, sink)
        prb = jnp.exp(s - m).astype(BF16)
        ov = _dot(prb, jnp.concatenate([vv, ones], axis=1))
        pick = lambda a: jnp.where(low, a[0:BLOCK], a[BLOCK:])
        den = pick(ov[:, LANES:])
        if use_sinks:
            den = den + pick(jnp.exp(sink - m))
        o_ref[qs, :] = (pick(ov[:, :LANES]) / den).astype(o_ref.dtype)
        if want_lse:
            lse_ref[qs, :] = pick(m) + jnp.log(den)
        return carry

    lax.fori_loop(0, nblk * dil, body, 0, unroll=ATTN_UNROLL)


def _band_attn(z, slopes, sinks, *, seq, q_col, k_col, v_col, n_pairs, dil, max_dist, gqa, use_sinks,
               want_lse, out_dtype):
    t = z.shape[0]
    bw = BLOCK * dil
    sb = min(max(2 * bw, 2048), seq)
    nblk = sb // bw
    qb, kb, vb = q_col // LANES, k_col // LANES, v_col // LANES
    if gqa:
        kv_map = lambda i, p: (i, kb)
        vv_map = lambda i, p: (i, vb)
        kp_map = lambda i, p: (jnp.maximum(i * nblk - 1, 0), kb)
        vp_map = lambda i, p: (jnp.maximum(i * nblk - 1, 0), vb)
    else:
        kv_map = lambda i, p: (i, kb + p)
        vv_map = lambda i, p: (i, vb + p)
        kp_map = lambda i, p: (jnp.maximum(i * nblk - 1, 0), kb + p)
        vp_map = lambda i, p: (jnp.maximum(i * nblk - 1, 0), vb + p)
    smem = pl.BlockSpec(memory_space=pltpu.SMEM)
    out_spec = pl.BlockSpec((sb, LANES), lambda i, p: (i, p))
    out_specs = [out_spec]
    out_shape = [jax.ShapeDtypeStruct((t, n_pairs * LANES), out_dtype)]
    if want_lse:
        out_specs.append(out_spec)
        out_shape.append(jax.ShapeDtypeStruct((t, n_pairs * LANES), F32))
    return pl.pallas_call(
        functools.partial(_band_attn_kernel, dil=dil, nblk=nblk, max_dist=max_dist, sb_per_seq=seq // sb,
                          gqa=gqa, use_sinks=use_sinks, want_lse=want_lse),
        grid=(t // sb, n_pairs),
        in_specs=[
            smem,
            smem,
            pl.BlockSpec((sb, LANES), lambda i, p: (i, qb + p)),
            pl.BlockSpec((sb, LANES), kv_map),
            pl.BlockSpec((sb, LANES), vv_map),
            pl.BlockSpec((bw, LANES), kp_map),
            pl.BlockSpec((bw, LANES), vp_map),
        ],
        out_specs=out_specs,
        out_shape=out_shape,
        scratch_shapes=[pltpu.VMEM((bw + sb, LANES), BF16 if dil == 1 else F32)] * 2
        + [pltpu.VMEM((2, 2 * BLOCK, 2 * BLOCK), F32)],
        compiler_params=_cparams("parallel", "arbitrary"),
        name=f"band_attn_d{dil}" + ("_gqa" if gqa else ""),
    )(slopes, sinks, z, z, z, z, z)


def _gelu(x):
    return 0.5 * x * (1.0 + lax.erf(x * np.float32(np.sqrt(0.5))))


def _gmlp_kernel(u0_ref, u1_ref, v0_ref, v1_ref, g_ref, b_ref, ws_ref, bs_ref, o_ref):
    rows = u0_ref.shape[0]
    u = _gelu(jnp.concatenate([u0_ref[...], u1_ref[...]], axis=1))
    v = _gelu(jnp.concatenate([v0_ref[...], v1_ref[...]], axis=1))
    v = _layer_norm(v, g_ref[...], b_ref[...]).astype(BF16)
    t_idx = lax.broadcasted_iota(jnp.int32, (B_CHUNK, B_CHUNK), 0)
    s_idx = lax.broadcasted_iota(jnp.int32, (B_CHUNK, B_CHUNK), 1)
    causal = t_idx >= s_idx
    for grp in range(B_GROUPS):
        w = jnp.where(causal, ws_ref[grp], 0.0).astype(BF16)
        bias = bs_ref[:, grp:grp + 1]
        cols = slice(grp * B_CHUNK, (grp + 1) * B_CHUNK)
        for ch in range(rows // B_CHUNK):
            rs = slice(ch * B_CHUNK, (ch + 1) * B_CHUNK)
            mixed = _dot(w, v[rs, cols]) + bias
            o_ref[rs, cols] = (u[rs, cols] * mixed).astype(o_ref.dtype)


def _gmlp(z, ln_g, ln_b, w_s, b_s_t):
    t = z.shape[0]
    rows = min(2048, t)
    half = B_WIDTH // 2
    ub, vb = COL_UB // half, COL_VB // half
    col = lambda j: pl.BlockSpec((rows, half), lambda i: (i, j))
    vec = pl.BlockSpec((1, B_WIDTH), lambda i: (0, 0))
    return pl.pallas_call(
        _gmlp_kernel,
        grid=(t // rows,),
        in_specs=[col(ub), col(ub + 1), col(vb), col(vb + 1), vec, vec,
                  pl.BlockSpec((B_GROUPS, B_CHUNK, B_CHUNK), lambda i: (0, 0, 0)),
                  pl.BlockSpec((B_CHUNK, B_GROUPS), lambda i: (0, 0))],
        out_specs=pl.BlockSpec((rows, B_WIDTH), lambda i: (i, 0)),
        out_shape=jax.ShapeDtypeStruct((t, B_WIDTH), BF16),
        compiler_params=_cparams("parallel"),
        name="spatial_gating",
    )(z, z, z, z, ln_g, ln_b, w_s, b_s_t)


def _hgrn_tables():
    n = HGRN_CHUNK
    t = np.arange(n)[:, None]
    i = np.arange(n)[None, :]
    tabs = []
    for h in HGRN_LEVELS:
        start = (t // h) * h
        upper = ((t // h) % 2) == 1
        q_tab = (i >= start) & (i <= t)
        k_tab = (i > t) & (i < start + h)
        tabs.append(np.where(upper, q_tab, k_tab))
    tabs.append(i <= t)
    tabs.append(i > t)
    w = np.concatenate(tabs, axis=0).astype(np.float32)
    w = np.concatenate([w, w], axis=1)
    s = np.arange(n)[None, :]
    x = np.bitwise_xor(t, s)
    level = np.where(s < t, np.floor(np.log2(np.maximum(x, 1))).astype(np.int32), np.where(s == t, -1, -2))
    return w, level.astype(np.int32)


def _hgrn_kernel(lbl_ref, ng_ref, w_ref, lvl_ref, q0_ref, q1_ref, f0_ref, f1_ref, i0_ref, i1_ref, g0_ref, g1_ref,
                 o_ref, st_ref, *, layer):
    n = HGRN_CHUNK

    @pl.when(pl.program_id(1) == 0)
    def _():
        st_ref[...] = jnp.zeros_like(st_ref)

    logits = lbl_ref[...]
    e = jnp.exp(logits - jnp.max(logits, axis=0, keepdims=True))
    probs = e / jnp.sum(e, axis=0, keepdims=True)
    lb = jnp.zeros((1, C_WIDTH), F32)
    for j in range(1, layer + 1):
        lb = lb + probs[j:j + 1, :]

    w = w_ref[...]
    lvl = lvl_ref[...]
    row = lax.broadcasted_iota(jnp.int32, (n, C_WIDTH), 0)
    ng = ng_ref[...]
    nl = len(HGRN_LEVELS)
    for ch in range(q0_ref.shape[0] // n):
        rs = slice(ch * n, (ch + 1) * n)
        both = lambda a_ref, b_ref: jnp.concatenate([a_ref[rs, :], b_ref[rs, :]], axis=1)
        z = both(f0_ref, f1_ref)
        q = both(q0_ref, q1_ref)
        f = lb + (1.0 - lb) * jax.nn.sigmoid(z)
        logf = jnp.log(jnp.maximum(f, F_MIN))
        kk = (1.0 - lb) * jax.nn.sigmoid(-z)
        vb = both(i0_ref, i1_ref).astype(BF16)
        l1 = logf.astype(BF16)
        l2 = (logf - l1.astype(F32)).astype(BF16)
        expo = _dot(w, jnp.concatenate([l1, l2], axis=0))
        xs = []
        for li, h in enumerate(HGRN_LEVELS):
            upper = (row & h) != 0
            xs.append((jnp.where(upper, q, kk) * jnp.exp(expo[li * n:(li + 1) * n])).astype(BF16))
        x1 = jnp.where((row & 1) != 0, q * jnp.maximum(f, F_MIN), kk).astype(BF16)
        q_in = (q * jnp.exp(expo[nl * n:(nl + 1) * n])).astype(BF16)
        k_out = (kk * jnp.exp(expo[(nl + 1) * n:(nl + 2) * n])).astype(BF16)
        carry = jnp.exp(expo[(nl + 1) * n - 1:(nl + 1) * n])
        qb, kb = q.astype(BF16), kk.astype(BF16)
        outs = []
        for hd in range(C_HEADS):
            cs = slice(hd * C_DIM, (hd + 1) * C_DIM)
            scores = jnp.where(lvl == -1, _dot_nt(qb[:, cs], kb[:, cs]), 0.0)
            scores = scores + jnp.where(lvl == 0, _dot_nt(x1[:, cs], x1[:, cs]), 0.0)
            for li, h in enumerate(HGRN_LEVELS):
                scores = scores + jnp.where(lvl == int(np.log2(h)), _dot_nt(xs[li][:, cs], xs[li][:, cs]), 0.0)
            st = st_ref[hd]
            o = _dot(scores.astype(BF16), vb[:, cs]) + _dot_nt(q_in[:, cs], st.astype(BF16))
            st_ref[hd] = st * carry[:, cs] + _dot_tn(vb[:, cs], k_out[:, cs])
            outs.append(o * lax.rsqrt(jnp.mean(o * o, axis=-1, keepdims=True) + LN_EPS))
        o = jnp.concatenate(outs, axis=1) * ng
        o_ref[rs, :] = (o * _sigmoid(both(g0_ref, g1_ref))).astype(o_ref.dtype)


def _hgrn(z, lb_logits, norm_g, layer, batch, seq):
    t = z.shape[0]
    rows = min(1024, seq)
    per_seq = seq // rows
    half = C_WIDTH // 2
    w, lvl = _hgrn_tables()
    col = lambda c0, j: pl.BlockSpec((rows, half), lambda b, c: (b * per_seq + c, c0 // half + j))
    const = lambda shape: pl.BlockSpec(shape, lambda b, c: (0,) * len(shape))
    return pl.pallas_call(
        functools.partial(_hgrn_kernel, layer=layer),
        grid=(batch, per_seq),
        in_specs=[const(lb_logits.shape), const(norm_g.shape), const(w.shape), const(lvl.shape)]
        + [col(c0, j) for c0 in (COL_QC, COL_FC, COL_IC, COL_GC) for j in (0, 1)],
        out_specs=pl.BlockSpec((rows, C_WIDTH), lambda b, c: (b * per_seq + c, 0)),
        out_shape=jax.ShapeDtypeStruct((t, C_WIDTH), BF16),
        scratch_shapes=[pltpu.VMEM((C_HEADS, C_DIM, C_DIM), F32)],
        compiler_params=_cparams("parallel", "arbitrary"),
        name="hgrn2",
    )(lb_logits, norm_g, jnp.asarray(w, BF16), jnp.asarray(lvl), *([z] * 8))


def _merge_kernel(x_ref, oa_ref, ob_ref, oc_ref, d0_ref, d1_ref, d2_ref, l0_ref, l1_ref, l2_ref, gt_ref,
                  wa_ref, wb_ref, wc_ref, wd_ref, wo_ref, g_ref, b_ref, o_ref, obf_ref, y_ref, *, alpha):
    d = x_ref.shape[1]
    i = pl.program_id(0)
    nt = pl.num_programs(0) - 1

    def finish_previous():
        o = _layer_norm(y_ref[...], g_ref[...], b_ref[...])
        o_ref[...] = o
        obf_ref[...] = o.astype(BF16)

    @pl.when(i == 0)
    def _():
        y_ref[...] = jnp.zeros_like(y_ref)

    @pl.when(i < nt)
    def _():
        finish_previous()
        l0, l1, l2 = l0_ref[...], l1_ref[...], l2_ref[...]
        m = jnp.maximum(jnp.maximum(l0, l1), l2)
        e0, e1, e2 = jnp.exp(l0 - m), jnp.exp(l1 - m), jnp.exp(l2 - m)
        inv = 1.0 / (e0 + e1 + e2)
        od = (e0 * inv) * d0_ref[...] + (e1 * inv) * d1_ref[...] + (e2 * inv) * d2_ref[...]
        merged = gt_ref[:, 0:d].astype(F32) * _dot(oa_ref[...], wa_ref[...])
        merged = merged + gt_ref[:, d:2 * d].astype(F32) * _dot(ob_ref[...], wb_ref[...])
        merged = merged + gt_ref[:, 2 * d:3 * d].astype(F32) * _dot(oc_ref[...], wc_ref[...])
        merged = merged + gt_ref[:, 3 * d:4 * d].astype(F32) * _dot(od.astype(BF16), wd_ref[...])
        y_ref[...] = alpha * x_ref[...] + _dot(merged.astype(BF16), wo_ref[...])

    @pl.when(i == nt)
    def _():
        finish_previous()


def _merge(x, oa, ob, oc, ods, lses, gates, wa, wb, wc, wd, wo, layer, ln_g, ln_b, alpha):
    t, d = x.shape
    tm = min(256, t)
    nt = t // tm
    row = lambda width: pl.BlockSpec((tm, width), lambda i: (jnp.minimum(i, nt - 1), 0))
    lag = pl.BlockSpec((tm, d), lambda i: (jnp.maximum(i - 1, 0), 0))
    wspec = lambda k: pl.BlockSpec((None, k, d), lambda i: (layer, 0, 0), pipeline_mode=pl.Buffered(1))
    vec = pl.BlockSpec((1, d), lambda i: (0, 0))
    return pl.pallas_call(
        functools.partial(_merge_kernel, alpha=alpha),
        grid=(nt + 1,),
        in_specs=[row(d), row(A_WIDTH), row(B_WIDTH), row(C_WIDTH)] + [row(D_GROUP_WIDTH)] * 6
        + [row(N_BRANCHES * d), wspec(A_WIDTH), wspec(B_WIDTH), wspec(C_WIDTH), wspec(D_GROUP_WIDTH), wspec(d),
           vec, vec],
        out_specs=[lag, lag],
        out_shape=[jax.ShapeDtypeStruct((t, d), F32), jax.ShapeDtypeStruct((t, d), BF16)],
        scratch_shapes=[pltpu.VMEM((tm, d), F32)],
        compiler_params=_cparams("arbitrary"),
        name="merge_out_ln",
    )(x, oa, ob, oc, *ods, *lses, gates, wa, wb, wc, wd, wo, ln_g, ln_b)


def _ple_kernel(xb_ref, p_ref, wg_ref, wp_ref, o_ref):
    pb = p_ref[...].astype(BF16)
    for c0 in range(0, o_ref.shape[1], EPILOGUE_PANEL):
        cs = slice(c0, c0 + EPILOGUE_PANEL)
        o_ref[:, cs] = _sigmoid(_dot(xb_ref[...], wg_ref[:, cs])) * _dot(pb, wp_ref[:, cs])


def _ple(xb, p, wg, wp, layer):
    t, d = xb.shape
    pd = p.shape[-1]
    tm = min(1024, t)
    tn = min(1024, d)
    return pl.pallas_call(
        _ple_kernel,
        grid=(d // tn, t // tm),
        in_specs=[
            pl.BlockSpec((tm, d), lambda n, m: (m, 0)),
            pl.BlockSpec((None, tm, pd), lambda n, m: (layer, m, 0)),
            pl.BlockSpec((None, d, tn), lambda n, m: (layer, 0, n)),
            pl.BlockSpec((None, pd, tn), lambda n, m: (layer, 0, n)),
        ],
        out_specs=pl.BlockSpec((tm, tn), lambda n, m: (m, n)),
        out_shape=jax.ShapeDtypeStruct((t, d), F32),
        compiler_params=_cparams("parallel", "arbitrary"),
        name="ple_gate",
    )(xb, p, wg, wp)


def _alibi_slopes(n):
    return jnp.exp2(-8.0 * jnp.arange(1, n + 1, dtype=F32) / n)


def kernel(x, p, ln_g, ln_b, ffn_w_gate, ffn_w_up, ffn_w_down, w_in, attn_sinks, gmlp_ln_g, gmlp_ln_b, gmlp_w_s,
           gmlp_b_s, hgrn_lb_logits, hgrn_norm_g, w_br_a, w_br_b, w_br_c, w_br_d, w_out, ple_w_proj, ple_w_gate):
    batch, seq, d = x.shape
    depth = ln_g.shape[0]
    t = batch * seq
    alpha = (2.0 * depth) ** 0.25
    assert seq % (BLOCK * D_PATTERNS[-1][1]) == 0, "the widest dilation needs whole 128-row residue blocks"
    assert w_in.shape[-1] == MIX_WIDTH + N_BRANCHES * d
    bf =lambda a: a.astype(BF16)
    wg, wu, wd = bf(ffn_w_gate), bf(ffn_w_up), bf(ffn_w_down)
    w_in_b = bf(w_in)
    wa, wb, wc, wdd, wo = bf(w_br_a), bf(w_br_b), bf(w_br_c), bf(w_br_d), bf(w_out)
    wpp, wpg = bf(ple_w_proj), bf(ple_w_gate)
    p2 = p.reshape(depth, t, p.shape[-1])
    slopes = _alibi_slopes(N_SOFTMAX_HEADS)
    no_sinks = jnp.zeros((D_HEADS,), F32)

    xf = x.reshape(t, d)
    for i in range(depth):
        xf, xb = _ffn(xf, wg, wu, wd, i, 0, ln_g[i, 0][None], ln_b[i, 0][None], None, alpha)
        z = _proj(xb, w_in_b, i, 0, MIX_WIDTH, F32, False)
        gates = _proj(xb, w_in_b, i, MIX_WIDTH, N_BRANCHES * d, BF16, True)
        oa = _band_attn(z, slopes[:A_Q_HEADS], attn_sinks[i], seq=seq, q_col=COL_QA, k_col=COL_KA, v_col=COL_VA,
                        n_pairs=A_Q_HEADS // 2, dil=1, max_dist=A_WINDOW - 1, gqa=True, use_sinks=True,
                        want_lse=False, out_dtype=BF16)[0]
        ob = _gmlp(z, gmlp_ln_g[i][None], gmlp_ln_b[i][None], gmlp_w_s[i], gmlp_b_s[i].T)
        oc = _hgrn(z, hgrn_lb_logits, hgrn_norm_g[i][None], i, batch, seq)
        ods, lses = [], []
        for g, (window, dil) in enumerate(D_PATTERNS):
            h0 = A_Q_HEADS + g * D_HEADS
            od, lse = _band_attn(z, slopes[h0:h0 + D_HEADS], no_sinks, seq=seq, q_col=COL_QD + g * D_GROUP_WIDTH,
                                 k_col=COL_KD + g * D_GROUP_WIDTH, v_col=COL_VD + g * D_GROUP_WIDTH,
                                 n_pairs=D_HEADS // 2, dil=dil, max_dist=window // dil, gqa=False, use_sinks=False,
                                 want_lse=True, out_dtype=F32)
            ods.append(od)
            lses.append(lse)
        xf, xb = _merge(xf, oa, ob, oc, ods, lses, gates, wa, wb, wc, wdd, wo, i, ln_g[i, 1][None], ln_b[i, 1][None],
                        alpha)
        ple = _ple(xb, p2, wpg, wpp, i)
        xf, xb = _ffn(xf, wg, wu, wd, i, 1, ln_g[i, 2][None], ln_b[i, 2][None], ple, alpha)
    return xf.reshape(batch, seq, d)
```

```python
import functools

import numpy as np

import jax
import jax.numpy as jnp
from jax import lax
from jax.experimental import pallas as pl
from jax.experimental.pallas import tpu as pltpu

F32 = jnp.float32
BF16 = jnp.bfloat16

LANES = 128
VMEM_LIMIT_BYTES = 60 * 1024 * 1024

HEAD_DIM = 64
BLOCK = 128
LN_EPS = 1e-5
NEG_BIG = -1e30
F_MIN = 1e-6
A_Q_HEADS = 8
A_KV_HEADS = 2
A_WINDOW = 128
B_GROUPS = 4
B_CHUNK = 128
C_HEADS = 4
C_DIM = 128
D_PATTERNS = ((128, 1), (512, 4), (2048, 16))
D_HEADS = 4
N_SOFTMAX_HEADS = A_Q_HEADS + len(D_PATTERNS) * D_HEADS
N_BRANCHES = 4
A_WIDTH = A_Q_HEADS * HEAD_DIM
A_KV_WIDTH = A_KV_HEADS * HEAD_DIM
B_WIDTH = B_GROUPS * B_CHUNK
C_WIDTH = C_HEADS * C_DIM
D_GROUP_WIDTH = D_HEADS * HEAD_DIM
D_WIDTH = len(D_PATTERNS) * D_GROUP_WIDTH
COL_QA = 0
COL_KA = COL_QA + A_WIDTH
COL_VA = COL_KA + A_KV_WIDTH
COL_UB = COL_VA + A_KV_WIDTH
COL_VB = COL_UB + B_WIDTH
COL_QC = COL_VB + B_WIDTH
COL_FC = COL_QC + C_WIDTH
COL_IC = COL_FC + C_WIDTH
COL_GC = COL_IC + C_WIDTH
COL_QD = COL_GC + C_WIDTH
COL_KD = COL_QD + D_WIDTH
COL_VD = COL_KD + D_WIDTH
MIX_WIDTH = COL_VD + D_WIDTH

EPILOGUE_PANEL = 256
FFN_RING_SLOTS = 3
ATTN_UNROLL = 16
HGRN_CHUNK = 128
HGRN_LEVELS = (64, 32, 16, 8, 4, 2)


def _cparams(*sem):
    return pltpu.CompilerParams(dimension_semantics=sem, vmem_limit_bytes=VMEM_LIMIT_BYTES)


def _layer_norm(v, g, b):
    mu = jnp.mean(v, axis=-1, keepdims=True)
    d = v - mu
    var = jnp.mean(d * d, axis=-1, keepdims=True)
    return d * lax.rsqrt(var + LN_EPS) * g + b


def _sigmoid(x):
    return 0.5 * jnp.tanh(0.5 * x) + 0.5


def _dot(a, b):
    return jnp.dot(a, b, preferred_element_type=F32)


def _dot_nt(a, b):
    return lax.dot_general(a, b, (((1,), (1,)), ((), ())), preferred_element_type=F32)


def _dot_tn(a, b):
    return lax.dot_general(a, b, (((0,), (0,)), ((), ())), preferred_element_type=F32)


def _ffn_kernel(x_ref, wg_hbm, wu_hbm, wd_hbm, g_ref, b_ref, *rest, alpha, has_extra, layer, which, nt, nf, tf):
    ahead = FFN_RING_SLOTS - 1
    if has_extra:
        e_ref, o_ref, ob_ref, xb_ref, acc_ref, wgu_buf, wd_buf, sem = rest
    else:
        o_ref, ob_ref, xb_ref, acc_ref, wgu_buf, wd_buf, sem = rest
    i = pl.program_id(0)
    base = i * nf

    def copies(pos):
        slot = pos % FFN_RING_SLOTS
        off = (pos % nf) * tf
        if not isinstance(off, int):
            off = pl.multiple_of(off, tf)
        gate_dst = wgu_buf.at[slot, :, pl.ds(0, tf)]
        up_dst = wgu_buf.at[slot, :, pl.ds(tf, tf)]
        return (
            pltpu.make_async_copy(wg_hbm.at[layer, which, :, pl.ds(off, tf)], gate_dst, sem.at[0, slot]),
            pltpu.make_async_copy(wu_hbm.at[layer, which, :, pl.ds(off, tf)], up_dst, sem.at[1, slot]),
            pltpu.make_async_copy(wd_hbm.at[layer, which, pl.ds(off, tf), :], wd_buf.at[slot], sem.at[2, slot]),
        )

    def fetch(pos):
        for c in copies(pos):
            c.wait()

        @pl.when(pos + ahead < nt * nf)
        def _():
            for c in copies(pos + ahead):
                c.start()

    def contribution(pos):
        slot = pos % FFN_RING_SLOTS
        gu = _dot(xb_ref[...], wgu_buf[slot])
        g, u = gu[:, :tf], gu[:, tf:]
        half_g = 0.5 * g
        h = (half_g * (jnp.tanh(half_g) + 1.0) * u).astype(BF16)
        return _dot(h, wd_buf[slot])

    def finish_previous():
        y = 0.5 * acc_ref[...]
        if has_extra:
            y = y + e_ref[...]
        o = _layer_norm(y, g_ref[...], b_ref[...])
        o_ref[...] = o
        ob_ref[...] = o.astype(BF16)

    @pl.when(i == 0)
    def _():
        acc_ref[...] = jnp.zeros_like(acc_ref)
        for pos in range(min(ahead, nt * nf)):
            for c in copies(pos):
                c.start()

    @pl.when(i < nt)
    def _():
        fetch(base)
        finish_previous()
        xb_ref[...] = x_ref[...].astype(BF16)
        acc_ref[...] = (2.0 * alpha) * x_ref[...] + contribution(base)

        def chunk(f, carry):
            fetch(base + f)
            acc_ref[...] = acc_ref[...] + contribution(base + f)
            return carry

        lax.fori_loop(1, nf, chunk, 0)

    @pl.when(i == nt)
    def _():
        finish_previous()


def _ffn(x, wg, wu, wd, layer, which, ln_g, ln_b, extra, alpha, tm=512, tf=512):
    t, d = x.shape
    ff = wg.shape[-1]
    tm = min(tm, t)
    tf = min(tf, ff)
    nt, nf = t // tm, ff // tf
    cur = pl.BlockSpec((tm, d), lambda i: (jnp.minimum(i, nt - 1), 0))
    lag = pl.BlockSpec((tm, d), lambda i: (jnp.maximum(i - 1, 0), 0))
    vec = pl.BlockSpec((1, d), lambda i: (0, 0))
    hbm = pl.BlockSpec(memory_space=pl.ANY)
    in_specs = [cur, hbm, hbm, hbm, vec, vec]
    args = [x, wg, wu, wd, ln_g, ln_b]
    if extra is not None:
        in_specs.append(lag)
        args.append(extra)
    return pl.pallas_call(
        functools.partial(_ffn_kernel, alpha=alpha, has_extra=extra is not None, layer=layer, which=which, nt=nt,
                          nf=nf, tf=tf),
        grid=(nt + 1,),
        in_specs=in_specs,
        out_specs=[lag, lag],
        out_shape=[jax.ShapeDtypeStruct((t, d), F32), jax.ShapeDtypeStruct((t, d), BF16)],
        scratch_shapes=[pltpu.VMEM((tm, d), BF16), pltpu.VMEM((tm, d), F32),
                        pltpu.VMEM((FFN_RING_SLOTS, d, 2 * tf), BF16), pltpu.VMEM((FFN_RING_SLOTS, tf, d), BF16),
                        pltpu.SemaphoreType.DMA((3, FFN_RING_SLOTS))],
        compiler_params=_cparams("arbitrary"),
        name="ffn_ln",
    )(*args)


def _proj_kernel(x_ref, w_ref, o_ref, *, act):
    z = _dot(x_ref[...], w_ref[...])
    if act:
        z = _sigmoid(z)
    o_ref[...] = z.astype(o_ref.dtype)


def _proj(xb, w, layer, col0, width, out_dtype, act):
    t, d = xb.shape
    tm = min(2048, t)
    tn = min(1024, width)
    off = col0 // tn
    return pl.pallas_call(
        functools.partial(_proj_kernel, act=act),
        grid=(width // tn, t // tm),
        in_specs=[
            pl.BlockSpec((tm, d), lambda n, m: (m, 0)),
            pl.BlockSpec((None, d, tn), lambda n, m: (layer, 0, n + off)),
        ],
        out_specs=pl.BlockSpec((tm, tn), lambda n, m: (m, n)),
        out_shape=jax.ShapeDtypeStruct((t, width), out_dtype),
        compiler_params=_cparams("parallel", "arbitrary"),
        name="in_proj_gates" if act else "in_proj_mix",
    )(xb, w)


def _band_attn_kernel(slope_ref, sink_ref, q_ref, k_ref, v_ref, kp_ref, vp_ref, *rest,
                      dil, nblk, max_dist, sb_per_seq, gqa, use_sinks, want_lse):
    if want_lse:
        o_ref, lse_ref, kbuf, vbuf, bias_ref = rest
    else:
        o_ref, kbuf, vbuf, bias_ref = rest
        lse_ref = None
    i = pl.program_id(0)
    p = pl.program_id(1)
    bw = BLOCK * dil
    first_in_seq = (i % sb_per_seq) == 0
    lane = lax.broadcasted_iota(jnp.int32, (1, LANES), 1)
    low = lane < HEAD_DIM

    def stage(cur_ref, prev_ref, buf):
        cur = cur_ref[...]
        prev = prev_ref[...]
        if gqa:
            kv_first = (p // 2) == 0

            def dup(a):
                r = pltpu.roll(a, HEAD_DIM, 1)
                return jnp.where(kv_first, jnp.where(low, a, r), jnp.where(low, r, a))

            cur, prev = dup(cur), dup(prev)
        buf[0:bw, :] = prev.astype(buf.dtype)
        buf[bw:, :] = cur.astype(buf.dtype)

    if gqa:
        @pl.when(p % 2 == 0)
        def _():
            stage(k_ref, kp_ref, kbuf)
            stage(v_ref, vp_ref, vbuf)
    else:
        stage(k_ref, kp_ref, kbuf)
        stage(v_ref, vp_ref, vbuf)

    q_off = lax.broadcasted_iota(jnp.int32, (2 * BLOCK, 2 * BLOCK), 0) % BLOCK
    k_off = lax.broadcasted_iota(jnp.int32, (2 * BLOCK, 2 * BLOCK), 1)
    dist = q_off + BLOCK - k_off
    in_band = (dist >= 0) & (dist <= max_dist)
    is_cur = k_off >= BLOCK
    dist_f = (jnp.maximum(dist, 0) * dil).astype(F32)
    top = lax.broadcasted_iota(jnp.int32, (2 * BLOCK, 1), 0) < BLOCK
    neg_slope = jnp.where(top, -slope_ref[2 * p], -slope_ref[2 * p + 1])
    bias = jnp.where(in_band, neg_slope * dist_f, NEG_BIG)
    bias_ref[0] = bias
    bias_ref[1] = jnp.where(is_cur, bias, NEG_BIG)
    if use_sinks:
        sink = jnp.where(top, sink_ref[2 * p], sink_ref[2 * p + 1])
    ones = jnp.ones((2 * BLOCK, LANES), BF16)

    def body(it, carry):
        j = it // dil
        c = it % dil
        row0 = j * bw + c
        if dil == 1:
            row0 = pl.multiple_of(row0, BLOCK)
            qs = pl.ds(row0, BLOCK)
            ks = pl.ds(row0, 2 * BLOCK)
        else:
            qs = pl.ds(row0, BLOCK, stride=dil)
            ks = pl.ds(row0, 2 * BLOCK, stride=dil)
        qc = q_ref[qs, :] * (HEAD_DIM ** -0.5)
        kk = kbuf[ks, :].astype(BF16)
        vv = vbuf[ks, :].astype(BF16)
        q2 = jnp.concatenate([jnp.where(low, qc, 0.0), jnp.where(low, 0.0, qc)], axis=0).astype(BF16)
        tbl = bias_ref[jnp.logical_and(j == 0, first_in_seq).astype(jnp.int32)]
        s = jnp.where(tbl > 0.5 * NEG_BIG, _dot_nt(q2, kk) + tbl, NEG_BIG)
        m = jnp.max(s, axis=-1, keepdims=True)
        if use_sinks:
            m = jnp.maximum(m, sink)
        prb = jnp.exp(s - m).astype(BF16)
        ov = _dot(prb, jnp.concatenate([vv, ones], axis=1))
        pick = lambda a: jnp.where(low, a[0:BLOCK], a[BLOCK:])
        den = pick(ov[:, LANES:])
        if use_sinks:
            den = den + pick(jnp.exp(sink - m))
        o_ref[qs, :] = (pick(ov[:, :LANES]) / den).astype(o_ref.dtype)
        if want_lse:
            lse_ref[qs, :] = pick(m) + jnp.log(den)
        return carry

    lax.fori_loop(0, nblk * dil, body, 0, unroll=ATTN_UNROLL)


def _band_attn(z, slopes, sinks, *, seq, q_col, k_col, v_col, n_pairs, dil, max_dist, gqa, use_sinks,
               want_lse, out_dtype):
    t = z.shape[0]
    bw = BLOCK * dil
    sb = min(max(2 * bw, 4096), seq)
    nblk = sb // bw
    qb, kb, vb = q_col // LANES, k_col // LANES, v_col // LANES
    if gqa:
        kv_map = lambda i, p: (i, kb)
        vv_map = lambda i, p: (i, vb)
        kp_map = lambda i, p: (jnp.maximum(i * nblk - 1, 0), kb)
        vp_map = lambda i, p: (jnp.maximum(i * nblk - 1, 0), vb)
    else:
        kv_map = lambda i, p: (i, kb + p)
        vv_map = lambda i, p: (i, vb + p)
        kp_map = lambda i, p: (jnp.maximum(i * nblk - 1, 0), kb + p)
        vp_map = lambda i, p: (jnp.maximum(i * nblk - 1, 0), vb + p)
    smem = pl.BlockSpec(memory_space=pltpu.SMEM)
    out_spec = pl.BlockSpec((sb, LANES), lambda i, p: (i, p))
    out_specs = [out_spec]
    out_shape = [jax.ShapeDtypeStruct((t, n_pairs * LANES), out_dtype)]
    if want_lse:
        out_specs.append(out_spec)
        out_shape.append(jax.ShapeDtypeStruct((t, n_pairs * LANES), F32))
    return pl.pallas_call(
        functools.partial(_band_attn_kernel, dil=dil, nblk=nblk, max_dist=max_dist, sb_per_seq=seq // sb,
                          gqa=gqa, use_sinks=use_sinks, want_lse=want_lse),
        grid=(t // sb, n_pairs),
        in_specs=[
            smem,
            smem,
            pl.BlockSpec((sb, LANES), lambda i, p: (i, qb + p)),
            pl.BlockSpec((sb, LANES), kv_map),
            pl.BlockSpec((sb, LANES), vv_map),
            pl.BlockSpec((bw, LANES), kp_map),
            pl.BlockSpec((bw, LANES), vp_map),
        ],
        out_specs=out_specs,
        out_shape=out_shape,
        scratch_shapes=[pltpu.VMEM((bw + sb, LANES), BF16 if dil == 1 else F32)] * 2
        + [pltpu.VMEM((2, 2 * BLOCK, 2 * BLOCK), F32)],
        compiler_params=_cparams("parallel", "arbitrary"),
        name=f"band_attn_d{dil}" + ("_gqa" if gqa else ""),
    )(slopes, sinks, z, z, z, z, z)


def _gelu(x):
    return 0.5 * x * (1.0 + lax.erf(x * np.float32(np.sqrt(0.5))))


def _gmlp_kernel(u0_ref, u1_ref, v0_ref, v1_ref, g_ref, b_ref, ws_ref, bs_ref, o_ref):
    rows = u0_ref.shape[0]
    u = _gelu(jnp.concatenate([u0_ref[...], u1_ref[...]], axis=1))
    v = _gelu(jnp.concatenate([v0_ref[...], v1_ref[...]], axis=1))
    v = _layer_norm(v, g_ref[...], b_ref[...]).astype(BF16)
    t_idx = lax.broadcasted_iota(jnp.int32, (B_CHUNK, B_CHUNK), 0)
    s_idx = lax.broadcasted_iota(jnp.int32, (B_CHUNK, B_CHUNK), 1)
    causal = t_idx >= s_idx
    for grp in range(B_GROUPS):
        w = jnp.where(causal, ws_ref[grp], 0.0).astype(BF16)
        bias = bs_ref[:, grp:grp + 1]
        cols = slice(grp * B_CHUNK, (grp + 1) * B_CHUNK)
        for ch in range(rows // B_CHUNK):
            rs = slice(ch * B_CHUNK, (ch + 1) * B_CHUNK)
            mixed = _dot(w, v[rs, cols]) + bias
            o_ref[rs, cols] = (u[rs, cols] * mixed).astype(o_ref.dtype)


def _gmlp(z, ln_g, ln_b, w_s, b_s_t):
    t = z.shape[0]
    rows = min(2048, t)
    half = B_WIDTH // 2
    ub, vb = COL_UB // half, COL_VB // half
    col = lambda j: pl.BlockSpec((rows, half), lambda i: (i, j))
    vec = pl.BlockSpec((1, B_WIDTH), lambda i: (0, 0))
    return pl.pallas_call(
        _gmlp_kernel,
        grid=(t // rows,),
        in_specs=[col(ub), col(ub + 1), col(vb), col(vb + 1), vec, vec,
                  pl.BlockSpec((B_GROUPS, B_CHUNK, B_CHUNK), lambda i: (0, 0, 0)),
                  pl.BlockSpec((B_CHUNK, B_GROUPS), lambda i: (0, 0))],
        out_specs=pl.BlockSpec((rows, B_WIDTH), lambda i: (i, 0)),
        out_shape=jax.ShapeDtypeStruct((t, B_WIDTH), BF16),
        compiler_params=_cparams("parallel"),
        name="spatial_gating",
    )(z, z, z, z, ln_g, ln_b, w_s, b_s_t)


def _hgrn_tables():
    n = HGRN_CHUNK
    t = np.arange(n)[:, None]
    i = np.arange(n)[None, :]
    tabs = []
    for h in HGRN_LEVELS:
        start = (t // h) * h
        upper = ((t // h) % 2) == 1
        q_tab = (i >= start) & (i <= t)
        k_tab = (i > t) & (i < start + h)
        tabs.append(np.where(upper, q_tab, k_tab))
    tabs.append(i <= t)
    tabs.append(i > t)
    w = np.concatenate(tabs, axis=0).astype(np.float32)
    w = np.concatenate([w, w], axis=1)
    s = np.arange(n)[None, :]
    x = np.bitwise_xor(t, s)
    level = np.where(s < t, np.floor(np.log2(np.maximum(x, 1))).astype(np.int32), np.where(s == t, -1, -2))
    return w, level.astype(np.int32)


def _hgrn_kernel(lbl_ref, ng_ref, w_ref, lvl_ref, q0_ref, q1_ref, f0_ref, f1_ref, i0_ref, i1_ref, g0_ref, g1_ref,
                 o_ref, st_ref, *, layer):
    n = HGRN_CHUNK

    @pl.when(pl.program_id(1) == 0)
    def _():
        st_ref[...] = jnp.zeros_like(st_ref)

    logits = lbl_ref[...]
    e = jnp.exp(logits - jnp.max(logits, axis=0, keepdims=True))
    probs = e / jnp.sum(e, axis=0, keepdims=True)
    lb = jnp.zeros((1, C_WIDTH), F32)
    for j in range(1, layer + 1):
        lb = lb + probs[j:j + 1, :]

    w = w_ref[...]
    lvl = lvl_ref[...]
    row = lax.broadcasted_iota(jnp.int32, (n, C_WIDTH), 0)
    ng = ng_ref[...]
    nl = len(HGRN_LEVELS)
    for ch in range(q0_ref.shape[0] // n):
        rs = slice(ch * n, (ch + 1) * n)
        both = lambda a_ref, b_ref: jnp.concatenate([a_ref[rs, :], b_ref[rs, :]], axis=1)
        z = both(f0_ref, f1_ref)
        q = both(q0_ref, q1_ref)
        f = lb + (1.0 - lb) * jax.nn.sigmoid(z)
        logf = jnp.log(jnp.maximum(f, F_MIN))
        kk = (1.0 - lb) * jax.nn.sigmoid(-z)
        vb = both(i0_ref, i1_ref).astype(BF16)
        l1 = logf.astype(BF16)
        l2 = (logf - l1.astype(F32)).astype(BF16)
        expo = _dot(w, jnp.concatenate([l1, l2], axis=0))
        xs = []
        for li, h in enumerate(HGRN_LEVELS):
            upper = (row & h) != 0
            xs.append((jnp.where(upper, q, kk) * jnp.exp(expo[li * n:(li + 1) * n])).astype(BF16))
        x1 = jnp.where((row & 1) != 0, q * jnp.maximum(f, F_MIN), kk).astype(BF16)
        q_in = (q * jnp.exp(expo[nl * n:(nl + 1) * n])).astype(BF16)
        k_out = (kk * jnp.exp(expo[(nl + 1) * n:(nl + 2) * n])).astype(BF16)
        carry = jnp.exp(expo[(nl + 1) * n - 1:(nl + 1) * n])
        qb, kb = q.astype(BF16), kk.astype(BF16)
        outs = []
        for hd in range(C_HEADS):
            cs = slice(hd * C_DIM, (hd + 1) * C_DIM)
            scores = jnp.where(lvl == -1, _dot_nt(qb[:, cs], kb[:, cs]), 0.0)
            scores = scores + jnp.where(lvl == 0, _dot_nt(x1[:, cs], x1[:, cs]), 0.0)
            for li, h in enumerate(HGRN_LEVELS):
                scores = scores + jnp.where(lvl == int(np.log2(h)), _dot_nt(xs[li][:, cs], xs[li][:, cs]), 0.0)
            st = st_ref[hd]
            o = _dot(scores.astype(BF16), vb[:, cs]) + _dot_nt(q_in[:, cs], st.astype(BF16))
            st_ref[hd] = st * carry[:, cs] + _dot_tn(vb[:, cs], k_out[:, cs])
            outs.append(o * lax.rsqrt(jnp.mean(o * o, axis=-1, keepdims=True) + LN_EPS))
        o = jnp.concatenate(outs, axis=1) * ng
        o_ref[rs, :] = (o * _sigmoid(both(g0_ref, g1_ref))).astype(o_ref.dtype)


def _hgrn(z, lb_logits, norm_g, layer, batch, seq):
    t = z.shape[0]
    rows = min(1024, seq)
    per_seq = seq // rows
    half = C_WIDTH // 2
    w, lvl = _hgrn_tables()
    col = lambda c0, j: pl.BlockSpec((rows, half), lambda b, c: (b * per_seq + c, c0 // half + j))
    const = lambda shape: pl.BlockSpec(shape, lambda b, c: (0,) * len(shape))
    return pl.pallas_call(
        functools.partial(_hgrn_kernel, layer=layer),
        grid=(batch, per_seq),
        in_specs=[const(lb_logits.shape), const(norm_g.shape), const(w.shape), const(lvl.shape)]
        + [col(c0, j) for c0 in (COL_QC, COL_FC, COL_IC, COL_GC) for j in (0, 1)],
        out_specs=pl.BlockSpec((rows, C_WIDTH), lambda b, c: (b * per_seq + c, 0)),
        out_shape=jax.ShapeDtypeStruct((t, C_WIDTH), BF16),
        scratch_shapes=[pltpu.VMEM((C_HEADS, C_DIM, C_DIM), F32)],
        compiler_params=_cparams("parallel", "arbitrary"),
        name="hgrn2",
    )(lb_logits, norm_g, jnp.asarray(w, BF16), jnp.asarray(lvl), *([z] * 8))


def _merge_kernel(x_ref, oa_ref, ob_ref, oc_ref, d0_ref, d1_ref, d2_ref, l0_ref, l1_ref, l2_ref, gt_ref,
                  wa_ref, wb_ref, wc_ref, wd_ref, wo_ref, g_ref, b_ref, o_ref, obf_ref, y_ref, *, alpha):
    d = x_ref.shape[1]
    i = pl.program_id(0)
    nt = pl.num_programs(0) - 1

    def finish_previous():
        o = _layer_norm(y_ref[...], g_ref[...], b_ref[...])
        o_ref[...] = o
        obf_ref[...] = o.astype(BF16)

    @pl.when(i == 0)
    def _():
        y_ref[...] = jnp.zeros_like(y_ref)

    @pl.when(i < nt)
    def _():
        finish_previous()
        l0, l1, l2 = l0_ref[...], l1_ref[...], l2_ref[...]
        m = jnp.maximum(jnp.maximum(l0, l1), l2)
        e0, e1, e2 = jnp.exp(l0 - m), jnp.exp(l1 - m), jnp.exp(l2 - m)
        inv = 1.0 / (e0 + e1 + e2)
        od = (e0 * inv) * d0_ref[...] + (e1 * inv) * d1_ref[...] + (e2 * inv) * d2_ref[...]
        merged = gt_ref[:, 0:d].astype(F32) * _dot(oa_ref[...], wa_ref[...])
        merged = merged + gt_ref[:, d:2 * d].astype(F32) * _dot(ob_ref[...], wb_ref[...])
        merged = merged + gt_ref[:, 2 * d:3 * d].astype(F32) * _dot(oc_ref[...], wc_ref[...])
        merged = merged + gt_ref[:, 3 * d:4 * d].astype(F32) * _dot(od.astype(BF16), wd_ref[...])
        y_ref[...] = alpha * x_ref[...] + _dot(merged.astype(BF16), wo_ref[...])

    @pl.when(i == nt)
    def _():
        finish_previous()


def _merge(x, oa, ob, oc, ods, lses, gates, wa, wb, wc, wd, wo, layer, ln_g, ln_b, alpha):
    t, d = x.shape
    tm = min(256, t)
    nt = t // tm
    row = lambda width: pl.BlockSpec((tm, width), lambda i: (jnp.minimum(i, nt - 1), 0))
    lag = pl.BlockSpec((tm, d), lambda i: (jnp.maximum(i - 1, 0), 0))
    wspec = lambda k: pl.BlockSpec((None, k, d), lambda i: (layer, 0, 0), pipeline_mode=pl.Buffered(1))
    vec = pl.BlockSpec((1, d), lambda i: (0, 0))
    return pl.pallas_call(
        functools.partial(_merge_kernel, alpha=alpha),
        grid=(nt + 1,),
        in_specs=[row(d), row(A_WIDTH), row(B_WIDTH), row(C_WIDTH)] + [row(D_GROUP_WIDTH)] * 6
        + [row(N_BRANCHES * d), wspec(A_WIDTH), wspec(B_WIDTH), wspec(C_WIDTH), wspec(D_GROUP_WIDTH), wspec(d),
           vec, vec],
        out_specs=[lag, lag],
        out_shape=[jax.ShapeDtypeStruct((t, d), F32), jax.ShapeDtypeStruct((t, d), BF16)],
        scratch_shapes=[pltpu.VMEM((tm, d), F32)],
        compiler_params=_cparams("arbitrary"),
        name="merge_out_ln",
    )(x, oa, ob, oc, *ods, *lses, gates, wa, wb, wc, wd, wo, ln_g, ln_b)


def _ple_kernel(xb_ref, p_ref, wg_ref, wp_ref, o_ref):
    pb = p_ref[...].astype(BF16)
    for c0 in range(0, o_ref.shape[1], EPILOGUE_PANEL):
        cs = slice(c0, c0 + EPILOGUE_PANEL)
        o_ref[:, cs] = _sigmoid(_dot(xb_ref[...], wg_ref[:, cs])) * _dot(pb, wp_ref[:, cs])


def _ple(xb, p, wg, wp, layer):
    t, d = xb.shape
    pd = p.shape[-1]
    tm = min(1024, t)
    tn = min(1024, d)
    return pl.pallas_call(
        _ple_kernel,
        grid=(d // tn, t // tm),
        in_specs=[
            pl.BlockSpec((tm, d), lambda n, m: (m, 0)),
            pl.BlockSpec((None, tm, pd), lambda n, m: (layer, m, 0)),
            pl.BlockSpec((None, d, tn), lambda n, m: (layer, 0, n)),
            pl.BlockSpec((None, pd, tn), lambda n, m: (layer, 0, n)),
        ],
        out_specs=pl.BlockSpec((tm, tn), lambda n, m: (m, n)),
        out_shape=jax.ShapeDtypeStruct((t, d), F32),
        compiler_params=_cparams("parallel", "arbitrary"),
        name="ple_gate",
    )(xb, p, wg, wp)


def _alibi_slopes(n):
    return jnp.exp2(-8.0 * jnp.arange(1, n + 1, dtype=F32) / n)


def kernel(x, p, ln_g, ln_b, ffn_w_gate, ffn_w_up, ffn_w_down, w_in, attn_sinks, gmlp_ln_g, gmlp_ln_b, gmlp_w_s,
           gmlp_b_s, hgrn_lb_logits, hgrn_norm_g, w_br_a, w_br_b, w_br_c, w_br_d, w_out, ple_w_proj, ple_w_gate):
    batch, seq, d = x.shape
    depth = ln_g.shape[0]
    t = batch * seq
    alpha = (2.0 * depth) ** 0.25
    assert seq % (BLOCK * D_PATTERNS[-1][1]) == 0, "the widest dilation needs whole 128-row residue blocks"
    assert w_in.shape[-1] == MIX_WIDTH + N_BRANCHES * d
    bf =lambda a: a.astype(BF16)
    wg, wu, wd = bf(ffn_w_gate), bf(ffn_w_up), bf(ffn_w_down)
    w_in_b = bf(w_in)
    wa, wb, wc, wdd, wo = bf(w_br_a), bf(w_br_b), bf(w_br_c), bf(w_br_d), bf(w_out)
    wpp, wpg = bf(ple_w_proj), bf(ple_w_gate)
    p2 = p.reshape(depth, t, p.shape[-1])
    slopes = _alibi_slopes(N_SOFTMAX_HEADS)
    no_sinks = jnp.zeros((D_HEADS,), F32)

    xf = x.reshape(t, d)
    for i in range(depth):
        xf, xb = _ffn(xf, wg, wu, wd, i, 0, ln_g[i, 0][None], ln_b[i, 0][None], None, alpha)
        z = _proj(xb, w_in_b, i, 0, MIX_WIDTH, F32, False)
        gates = _proj(xb, w_in_b, i, MIX_WIDTH, N_BRANCHES * d, BF16, True)
        oa = _band_attn(z, slopes[:A_Q_HEADS], attn_sinks[i], seq=seq, q_col=COL_QA, k_col=COL_KA, v_col=COL_VA,
                        n_pairs=A_Q_HEADS // 2, dil=1, max_dist=A_WINDOW - 1, gqa=True, use_sinks=True,
                        want_lse=False, out_dtype=BF16)[0]
        ob = _gmlp(z, gmlp_ln_g[i][None], gmlp_ln_b[i][None], gmlp_w_s[i], gmlp_b_s[i].T)
        oc = _hgrn(z, hgrn_lb_logits, hgrn_norm_g[i][None], i, batch, seq)
        ods, lses = [], []
        for g, (window, dil) in enumerate(D_PATTERNS):
            h0 = A_Q_HEADS + g * D_HEADS
            od, lse = _band_attn(z, slopes[h0:h0 + D_HEADS], no_sinks, seq=seq, q_col=COL_QD + g * D_GROUP_WIDTH,
                                 k_col=COL_KD + g * D_GROUP_WIDTH, v_col=COL_VD + g * D_GROUP_WIDTH,
                                 n_pairs=D_HEADS // 2, dil=dil, max_dist=window // dil, gqa=False, use_sinks=False,
                                 want_lse=True, out_dtype=F32)
            ods.append(od)
            lses.append(lse)
        xf, xb = _merge(xf, oa, ob, oc, ods, lses, gates, wa, wb, wc, wdd, wo, i, ln_g[i, 1][None], ln_b[i, 1][None],
                        alpha)
        ple = _ple(xb, p2, wpg, wpp, i)
        xf, xb = _ffn(xf, wg, wu, wd, i, 1, ln_g[i, 2][None], ln_b[i, 2][None], ple, alpha)
    return xf.reshape(batch, seq, d)
```
